```python
import jax, jax.numpy as jnp
from jax import lax
import numpy as np

D_MODEL = 2048
BATCH = 4
SEQ = 4096
DEPTH = 4

N_MIXERS = 2
HEAD_DIM = 128
N_ATTN_HEADS = 12
ATTN_WIDTH = N_ATTN_HEADS * HEAD_DIM
Q_LORA = 512
KV_LORA = 256
IDX_HEADS = 16
IDX_DIM = 64
TOPK_MAX = 256
Q_BLOCK = 128
POOL_WINDOWS = (2, 4, 8, 16)
N_POOL_GROUPS = len(POOL_WINDOWS)
POOL_WIDTH = 1536
POOL_GROUP = POOL_WIDTH // N_POOL_GROUPS
N_MEM = 256
N_MEM_HEADS = 4
MEM_WIDTH = N_MEM_HEADS * HEAD_DIM
MIX_WIDTH = ATTN_WIDTH + MEM_WIDTH
D_FF = -(-8 * D_MODEL // (3 * 256)) * 256
ATTN_SPLITS = (Q_LORA, KV_LORA, IDX_DIM, IDX_HEADS, MEM_WIDTH)
ATTN_IN = sum(ATTN_SPLITS)
POOL_IN = POOL_WIDTH + MEM_WIDTH
N_ATTN_LAYERS = (DEPTH + 1) // 2
N_POOL_LAYERS = DEPTH // 2
EPS = 1e-6

kernel_name = "dsa_pool_hybrid_trunk"


def rmsnorm(x, g):
    xf = x.astype(jnp.float32)
    y = xf * lax.rsqrt(jnp.mean(xf * xf, axis=-1, keepdims=True) + EPS)
    return (y * g.astype(jnp.float32)).astype(x.dtype)


def layernorm(x, g, b):
    xf = x.astype(jnp.float32)
    mu = jnp.mean(xf, axis=-1, keepdims=True)
    var = jnp.mean(jnp.square(xf - mu), axis=-1, keepdims=True)
    y = (xf - mu) * lax.rsqrt(var + EPS)
    return (y * g.astype(jnp.float32) + b.astype(jnp.float32)).astype(x.dtype)


def split_cols(a, sizes):
    offs = np.cumsum(sizes)[:-1].tolist()
    return jnp.split(a, offs, axis=-1)


def dsa_mixer(h, w_in, q_norm, kv_norm, w_uq, w_uk, w_uv, w_idx_uq, idx_k_norm, idx_k_bias):
    B, T, _ = h.shape
    c_q, c_kv, k_idx, w_idx, q_mem = split_cols(h @ w_in, ATTN_SPLITS)
    c_q = rmsnorm(c_q, q_norm)
    c_kv = rmsnorm(c_kv, kv_norm)
    k_idx = layernorm(k_idx, idx_k_norm, idx_k_bias)
    q = (c_q @ w_uq).reshape(B, T, N_ATTN_HEADS, HEAD_DIM)
    q_abs = jnp.einsum('bthd,hcd->bthc', q, w_uk)
    q_idx = (c_q @ w_idx_uq).reshape(B, T, IDX_HEADS, IDX_DIM)
    w_idx = w_idx * (IDX_HEADS ** -0.5 * IDX_DIM ** -0.5)
    k_top = min(TOPK_MAX, T // 4)
    n_blk = T // Q_BLOCK
    key_pos = jnp.arange(T)
    scale = HEAD_DIM ** -0.5

    def block(i):
        t0 = i * Q_BLOCK
        qi = lax.dynamic_slice_in_dim(q_idx, t0, Q_BLOCK, 1)
        wi = lax.dynamic_slice_in_dim(w_idx, t0, Q_BLOCK, 1)
        qa = lax.dynamic_slice_in_dim(q_abs, t0, Q_BLOCK, 1)
        qpos = t0 + jnp.arange(Q_BLOCK)
        s = jax.nn.relu(jnp.einsum('bqhd,bsd->bqhs', qi, k_idx))
        score = jnp.einsum('bqh,bqhs->bqs', wi, s).astype(jnp.float32)
        causal = key_pos[None, :] <= qpos[:, None]
        score = jnp.where(causal[None], score, -jnp.inf)
        _, sel = lax.top_k(score, k_top)
        valid = sel <= qpos[None, :, None]
        c_sel = jax.vmap(lambda c, ix: c[ix])(c_kv, sel)
        logits = jnp.einsum('bqhc,bqkc->bqhk', qa, c_sel).astype(jnp.float32) * scale
        logits = jnp.where(valid[:, :, None, :], logits, -jnp.inf)
        p = jax.nn.softmax(logits, axis=-1).astype(c_sel.dtype)
        return jnp.einsum('bqhk,bqkc->bqhc', p, c_sel)

    o_lat = lax.map(block, jnp.arange(n_blk))
    o_lat = o_lat.transpose(1, 0, 2, 3, 4).reshape(B, T, N_ATTN_HEADS, KV_LORA)
    o = jnp.einsum('bthc,hcd->bthd', o_lat, w_uv).reshape(B, T, ATTN_WIDTH)
    return o, q_mem


def pool_mixer(h, w_in, w_group, scale):
    B, T, _ = h.shape
    proj = h @ w_in
    u, q_mem = proj[..., :POOL_WIDTH], proj[..., POOL_WIDTH:]
    uf = u.astype(jnp.float32)
    cs = jnp.concatenate([jnp.zeros((B, 1, POOL_WIDTH), jnp.float32), jnp.cumsum(uf, axis=1)], axis=1)
    pos = jnp.arange(T)
    outs = []
    for g, w in enumerate(POOL_WINDOWS):
        sl = slice(g * POOL_GROUP, (g + 1) * POOL_GROUP)
        seg = cs[..., sl]
        start = jnp.maximum(pos + 1 - w, 0)
        total = jnp.take(seg, pos + 1, axis=1) - jnp.take(seg, start, axis=1)
        cnt = (pos + 1 - start).astype(jnp.float32)
        outs.append(total / cnt[None, :, None] - uf[..., sl])
    y = jnp.stack(outs, axis=2).astype(u.dtype)
    y = jnp.einsum('btgc,gcd->btgd', y, w_group).reshape(B, T, POOL_WIDTH) * scale
    return y, q_mem


def mem_attend(q_mem, mem_k, mem_v):
    B, T = q_mem.shape[:2]
    logits = jnp.einsum('bthd,bnhd->bhtn', q_mem, mem_k).astype(jnp.float32) * HEAD_DIM ** -0.5
    p = jax.nn.softmax(logits, axis=-1).astype(mem_v.dtype)
    return jnp.einsum('bhtn,bnhd->bthd', p, mem_v).reshape(B, T, MEM_WIDTH)


def setup_inputs(seed: int = 0) -> dict:
    key = jax.random.key(seed)
    ks = iter(jax.random.split(key, 32))

    def nrm(shape, fan_in):
        return jax.random.normal(next(ks), shape, jnp.float32) * fan_in ** -0.5

    def gain(shape):
        return 1.0 + 0.1 * jax.random.normal(next(ks), shape, jnp.float32)

    return {
        "x": jax.random.normal(next(ks), (BATCH, SEQ, D_MODEL), jnp.float32),
        "mem": jax.random.normal(next(ks), (BATCH, N_MEM, D_MODEL), jnp.float32),
        "mixer_norm": gain((DEPTH, D_MODEL)),
        "ffn_norm": gain((DEPTH, D_MODEL)),
        "final_norm": gain((D_MODEL,)),
        "mem_norm": gain((D_MODEL,)),
        "w_mem_kv": nrm((DEPTH, D_MODEL, 2 * MEM_WIDTH), D_MODEL),
        "w_out": nrm((DEPTH, MIX_WIDTH, D_MODEL), MIX_WIDTH),
        "w_gate": nrm((DEPTH, D_MODEL, D_FF), D_MODEL),
        "w_up": nrm((DEPTH, D_MODEL, D_FF), D_MODEL),
        "w_down": nrm((DEPTH, D_FF, D_MODEL), D_FF),
        "w_in_attn": nrm((N_ATTN_LAYERS, D_MODEL, ATTN_IN), D_MODEL),
        "q_norm": gain((N_ATTN_LAYERS, Q_LORA)),
        "kv_norm": gain((N_ATTN_LAYERS, KV_LORA)),
        "w_uq": nrm((N_ATTN_LAYERS, Q_LORA, ATTN_WIDTH), Q_LORA),
        "w_uk": nrm((N_ATTN_LAYERS, N_ATTN_HEADS, KV_LORA, HEAD_DIM), HEAD_DIM),
        "w_uv": nrm((N_ATTN_LAYERS, N_ATTN_HEADS, KV_LORA, HEAD_DIM), KV_LORA),
        "w_idx_uq": nrm((N_ATTN_LAYERS, Q_LORA, IDX_HEADS * IDX_DIM), Q_LORA),
        "idx_k_norm": gain((N_ATTN_LAYERS, IDX_DIM)),
        "idx_k_bias": 0.02 * jax.random.normal(next(ks), (N_ATTN_LAYERS, IDX_DIM), jnp.float32),
        "w_in_pool": nrm((N_POOL_LAYERS, D_MODEL, POOL_IN), D_MODEL),
        "w_pool_group": nrm((N_POOL_LAYERS, N_POOL_GROUPS, POOL_GROUP, POOL_GROUP), POOL_GROUP),
        "pool_scale": gain((N_POOL_LAYERS, POOL_WIDTH)),
    }


def reference(x, mem, mixer_norm, ffn_norm, final_norm, mem_norm, w_mem_kv, w_out, w_gate, w_up, w_down,
              w_in_attn, q_norm, kv_norm, w_uq, w_uk, w_uv, w_idx_uq, idx_k_norm, idx_k_bias,
              w_in_pool, w_pool_group, pool_scale):
    B, T, _ = x.shape
    mem_n = rmsnorm(mem, mem_norm)
    for i in range(DEPTH):
        h = rmsnorm(x, mixer_norm[i])
        j = i // N_MIXERS
        if i % N_MIXERS == 0:
            y, q_mem = dsa_mixer(h, w_in_attn[j], q_norm[j], kv_norm[j], w_uq[j], w_uk[j], w_uv[j],
                                 w_idx_uq[j], idx_k_norm[j], idx_k_bias[j])
        else:
            y, q_mem = pool_mixer(h, w_in_pool[j], w_pool_group[j], pool_scale[j])
        kv = mem_n @ w_mem_kv[i]
        mk = kv[..., :MEM_WIDTH].reshape(B, N_MEM, N_MEM_HEADS, HEAD_DIM)
        mv = kv[..., MEM_WIDTH:].reshape(B, N_MEM, N_MEM_HEADS, HEAD_DIM)
        m = mem_attend(q_mem.reshape(B, T, N_MEM_HEADS, HEAD_DIM), mk, mv)
        x = x + jnp.concatenate([y, m], axis=-1) @ w_out[i]
        h = rmsnorm(x, ffn_norm[i])
        x = x + (jax.nn.silu(h @ w_gate[i]) * (h @ w_up[i])) @ w_down[i]
    return rmsnorm(x, final_norm)
```

```python
import functools

import jax
import jax.numpy as jnp
from jax import lax
from jax.experimental import pallas as pl
from jax.experimental.pallas import tpu as pltpu

BF16 = jnp.bfloat16
F32 = jnp.float32
I32 = jnp.int32

EPS = 1e-6
HEAD_DIM = 128
N_ATTN_HEADS = 12
Q_LORA = 512
KV_LORA = 256
IDX_HEADS = 16
IDX_DIM = 64
TOPK_MAX = 256
POOL_WINDOWS = (2, 4, 8, 16)
POOL_GROUP = 384
POOL_WIDTH = 1536
MEM_WIDTH = 512
N_MEM_HEADS = 4
POOL_HALO = 16

_CQ0, _CKV0, _QM0, _KI0, _WI0 = 0, 512, 768, 1280, 1344
ATTN_IN_PAD = 1408

LOG2E = 1.4426950408889634
INT_MIN = -(2 ** 31)
NEG = -1e30

VMEM_LIMIT = 56 * 1024 * 1024


def _cparams(*sem):
    return pltpu.CompilerParams(dimension_semantics=sem, vmem_limit_bytes=VMEM_LIMIT)


def _rms(xf, g):
    ms = jnp.mean(xf * xf, axis=-1, keepdims=True)
    return xf * lax.rsqrt(ms + EPS) * g


def _rms_matmul_kernel(x_ref, g_ref, w_ref, o_ref):
    h = _rms(x_ref[...], g_ref[...]).astype(BF16)
    o_ref[...] = jnp.dot(h, w_ref[...], preferred_element_type=F32).astype(o_ref.dtype)


def rms_matmul(x, g, w, *, tm, tn, out_dtype):
    m, d = x.shape
    n = w.shape[1]
    return pl.pallas_call(
        _rms_matmul_kernel,
        grid=(m // tm, n // tn),
        in_specs=[pl.BlockSpec((tm, d), lambda i, j: (i, 0)),
                  pl.BlockSpec((1, d), lambda i, j: (0, 0)),
                  pl.BlockSpec((d, tn), lambda i, j: (0, j))],
        out_specs=pl.BlockSpec((tm, tn), lambda i, j: (i, j)),
        out_shape=jax.ShapeDtypeStruct((m, n), out_dtype),
        compiler_params=_cparams("parallel", "parallel"),
        name="rms_matmul",
    )(x, g, w)


def _ffn_kernel(x_ref, g_ref, wg_ref, wu_ref, wd_ref, fg_ref, o_ref, h_scr, *, final_norm):
    f = pl.program_id(1)

    @pl.when(f == 0)
    def _():
        x = x_ref[...]
        h_scr[...] = _rms(x, g_ref[...]).astype(BF16)
        o_ref[...] = x

    h = h_scr[...]
    gate = jnp.dot(h, wg_ref[...], preferred_element_type=F32)
    up = jnp.dot(h, wu_ref[...], preferred_element_type=F32)
    act = (gate * jax.nn.sigmoid(gate) * up).astype(BF16)
    o_ref[...] += jnp.dot(act, wd_ref[...], preferred_element_type=F32)

    if final_norm:
        @pl.when(f == pl.num_programs(1) - 1)
        def _():
            o_ref[...] = _rms(o_ref[...], fg_ref[...])


def ffn(x, g, w_gate, w_up, w_down, final_g, *, tm, tf, final_norm):
    m, d = x.shape
    dff = w_gate.shape[1]
    return pl.pallas_call(
        functools.partial(_ffn_kernel, final_norm=final_norm),
        grid=(m // tm, dff // tf),
        in_specs=[pl.BlockSpec((tm, d), lambda i, f: (i, 0)),
                  pl.BlockSpec((1, d), lambda i, f: (0, 0)),
                  pl.BlockSpec((d, tf), lambda i, f: (0, f)),
                  pl.BlockSpec((d, tf), lambda i, f: (0, f)),
                  pl.BlockSpec((tf, d), lambda i, f: (f, 0)),
                  pl.BlockSpec((1, d), lambda i, f: (0, 0))],
        out_specs=pl.BlockSpec((tm, d), lambda i, f: (i, 0)),
        out_shape=jax.ShapeDtypeStruct((m, d), F32),
        scratch_shapes=[pltpu.VMEM((tm, d), BF16)],
        compiler_params=_cparams("parallel", "arbitrary"),
        name="ffn",
    )(x, g, w_gate, w_up, w_down, final_g)


def _attn_proj_kernel(x_ref, g_ref, win_ref, qn_ref, kvn_ref, ig_ref, ib_ref, wuq_ref, wukt_ref, wiq_ref,
                      ckv_ref, ckvt_ref, kidxt_ref, qmem_ref, qabs_ref, qidx_ref, widx_ref):
    h = _rms(x_ref[0], g_ref[...]).astype(BF16)
    proj = jnp.dot(h, win_ref[...], preferred_element_type=F32)
    c_q = _rms(proj[:, _CQ0:_CQ0 + Q_LORA], qn_ref[...]).astype(BF16)
    c_kv = _rms(proj[:, _CKV0:_CKV0 + KV_LORA], kvn_ref[...])
    ckv_ref[0] = c_kv.astype(BF16)
    ckvt_ref[0] = c_kv.T.astype(BF16)
    qmem_ref[0] = proj[:, _QM0:_QM0 + MEM_WIDTH].astype(BF16)
    ki2 = proj[:, _KI0:_KI0 + 128]
    lane = lax.broadcasted_iota(I32, ki2.shape, 1)
    live = lane < IDX_DIM
    mu = jnp.sum(jnp.where(live, ki2, 0.0), axis=-1, keepdims=True) * (1.0 / IDX_DIM)
    cen = jnp.where(live, ki2 - mu, 0.0)
    var = jnp.sum(cen * cen, axis=-1, keepdims=True) * (1.0 / IDX_DIM)
    kin = cen * lax.rsqrt(var + EPS) * ig_ref[...] + ib_ref[...]
    kidxt_ref[0] = kin.T[:IDX_DIM, :].astype(BF16)
    widx_ref[0] = ki2[:, IDX_DIM:IDX_DIM + IDX_HEADS] * (IDX_HEADS ** -0.5 * IDX_DIM ** -0.5)
    q = jnp.dot(c_q, wuq_ref[...], preferred_element_type=F32).astype(BF16)
    for hd in range(N_ATTN_HEADS):
        qa = jnp.dot(q[:, hd * HEAD_DIM:(hd + 1) * HEAD_DIM], wukt_ref[hd], preferred_element_type=F32)
        qabs_ref[0, hd] = (qa * (HEAD_DIM ** -0.5 * LOG2E)).astype(BF16)
    qi = jnp.dot(c_q, wiq_ref[...], preferred_element_type=F32)
    for hd in range(IDX_HEADS):
        qidx_ref[0, hd] = qi[:, hd * IDX_DIM:(hd + 1) * IDX_DIM].astype(BF16)


def attn_proj(x, g, w_in, q_norm, kv_norm, idx_g, idx_b, w_uq, w_ukt, w_iq, *, tm):
    b, t, d = x.shape
    full = lambda shape: pl.BlockSpec(shape, lambda bi, i: (0,) * len(shape))
    return pl.pallas_call(
        _attn_proj_kernel,
        grid=(b, t // tm),
        in_specs=[pl.BlockSpec((1, tm, d), lambda bi, i: (bi, i, 0)),
                  full((1, d)), full(w_in.shape), full((1, Q_LORA)), full((1, KV_LORA)),
                  full((1, 128)), full((1, 128)), full(w_uq.shape), full(w_ukt.shape), full(w_iq.shape)],
        out_specs=[pl.BlockSpec((1, tm, KV_LORA), lambda bi, i: (bi, i, 0)),
                   pl.BlockSpec((1, KV_LORA, tm), lambda bi, i: (bi, 0, i)),
                   pl.BlockSpec((1, IDX_DIM, tm), lambda bi, i: (bi, 0, i)),
                   pl.BlockSpec((1, tm, MEM_WIDTH), lambda bi, i: (bi, i, 0)),
                   pl.BlockSpec((1, N_ATTN_HEADS, tm, KV_LORA), lambda bi, i: (bi, 0, i, 0)),
                   pl.BlockSpec((1, IDX_HEADS, tm, IDX_DIM), lambda bi, i: (bi, 0, i, 0)),
                   pl.BlockSpec((1, tm, IDX_HEADS), lambda bi, i: (bi, i, 0))],
        out_shape=[jax.ShapeDtypeStruct((b, t, KV_LORA), BF16),
                   jax.ShapeDtypeStruct((b, KV_LORA, t), BF16),
                   jax.ShapeDtypeStruct((b, IDX_DIM, t), BF16),
                   jax.ShapeDtypeStruct((b, t, MEM_WIDTH), BF16),
                   jax.ShapeDtypeStruct((b, N_ATTN_HEADS, t, KV_LORA), BF16),
                   jax.ShapeDtypeStruct((b, IDX_HEADS, t, IDX_DIM), BF16),
                   jax.ShapeDtypeStruct((b, t, IDX_HEADS), F32)],
        compiler_params=_cparams("parallel", "parallel"),
        name="attn_proj",
    )(x, g, w_in, q_norm, kv_norm, idx_g, idx_b, w_uq, w_ukt, w_iq)


def _dsa_kernel(qidx_ref, widx_ref, qabs_ref, kidxt_ref, ckv_ref, ckvt_ref, wuv_ref, y_ref,
                key_scr, m_scr, l_scr, acc_scr, *, tq, tk, k_top):
    i = pl.program_id(1)
    q0 = i * tq
    n_chunks = (q0 + tq + tk - 1) // tk
    row = q0 + lax.broadcasted_iota(I32, (tq, 1), 0)

    qi = qidx_ref[0].reshape(IDX_HEADS * tq, IDX_DIM)
    w = widx_ref[0]

    def score_body(c, carry):
        k0 = pl.multiple_of(c * tk, tk)
        s = jnp.dot(qi, kidxt_ref[0, :, pl.ds(k0, tk)], preferred_element_type=F32)
        s = jnp.maximum(s, 0.0).reshape(IDX_HEADS, tq, tk)
        score = s[0] * w[:, 0:1]
        for hd in range(1, IDX_HEADS):
            score = score + s[hd] * w[:, hd:hd + 1]
        bits = pltpu.bitcast(score, I32)
        key = bits ^ ((bits >> 31) & 0x7FFFFFFF)
        col = k0 + lax.broadcasted_iota(I32, (1, tk), 1)
        key_scr[:, pl.ds(k0, tk)] = jnp.where(col <= row, key, INT_MIN)
        return carry

    lax.fori_loop(0, n_chunks, score_body, 0)

    def count_ge(cand):
        def body(c, acc):
            k0 = pl.multiple_of(c * tk, tk)
            hit = jnp.where(key_scr[:, pl.ds(k0, tk)] >= cand, 1, 0)
            for s in range(tk // 128):
                acc = acc + hit[:, s * 128:(s + 1) * 128]
            return acc
        acc = lax.fori_loop(0, n_chunks, body, jnp.zeros((tq, 128), I32))
        return jnp.sum(acc, axis=-1, keepdims=True)

    def bit_body(b, thr):
        cand = thr + lax.shift_left(jnp.int32(1), 31 - b)
        return jnp.where(count_ge(cand) >= k_top, cand, thr)

    thr = lax.fori_loop(0, 32, bit_body, jnp.full((tq, 1), INT_MIN, I32))
    thr = jnp.maximum(thr, INT_MIN + 1)

    qa = qabs_ref[0].reshape(N_ATTN_HEADS * tq, KV_LORA)
    m_scr[...] = jnp.full(m_scr.shape, NEG, F32)
    l_scr[...] = jnp.zeros(l_scr.shape, F32)
    acc_scr[...] = jnp.zeros(acc_scr.shape, F32)

    def att_body(c, carry):
        k0 = pl.multiple_of(c * tk, tk)
        logits = jnp.dot(qa, ckvt_ref[0, :, pl.ds(k0, tk)], preferred_element_type=F32)
        sel = key_scr[:, pl.ds(k0, tk)] >= thr
        lg = jnp.where(sel[None], logits.reshape(N_ATTN_HEADS, tq, tk), NEG)
        m_prev = m_scr[...]
        m_new = jnp.maximum(m_prev, jnp.max(lg, axis=-1, keepdims=True))
        alpha = jnp.exp2(m_prev - m_new)
        p = jnp.exp2(lg - m_new)
        l_scr[...] = alpha * l_scr[...] + jnp.sum(p, axis=-1, keepdims=True)
        pv = jnp.dot(p.astype(BF16).reshape(N_ATTN_HEADS * tq, tk), ckv_ref[0, pl.ds(k0, tk), :],
                     preferred_element_type=F32)
        acc_scr[...] = alpha * acc_scr[...] + pv.reshape(N_ATTN_HEADS, tq, KV_LORA)
        m_scr[...] = m_new
        return carry

    lax.fori_loop(0, n_chunks, att_body, 0)

    o_lat = (acc_scr[...] / l_scr[...]).astype(BF16)
    for hd in range(N_ATTN_HEADS):
        y_ref[0, :, hd * HEAD_DIM:(hd + 1) * HEAD_DIM] = jnp.dot(
            o_lat[hd], wuv_ref[hd], preferred_element_type=F32).astype(y_ref.dtype)


def dsa_attention(q_idx, w_idx, q_abs, k_idx_t, c_kv, c_kv_t, w_uv, *, tq, tk):
    b, _, t, _ = q_abs.shape
    k_top = min(TOPK_MAX, t // 4)
    return pl.pallas_call(
        functools.partial(_dsa_kernel, tq=tq, tk=tk, k_top=k_top),
        grid=(b, t // tq),
        in_specs=[pl.BlockSpec((1, IDX_HEADS, tq, IDX_DIM), lambda bi, i: (bi, 0, i, 0)),
                  pl.BlockSpec((1, tq, IDX_HEADS), lambda bi, i: (bi, i, 0)),
                  pl.BlockSpec((1, N_ATTN_HEADS, tq, KV_LORA), lambda bi, i: (bi, 0, i, 0)),
                  pl.BlockSpec((1, IDX_DIM, t), lambda bi, i: (bi, 0, 0)),
                  pl.BlockSpec((1, t, KV_LORA), lambda bi, i: (bi, 0, 0)),
                  pl.BlockSpec((1, KV_LORA, t), lambda bi, i: (bi, 0, 0)),
                  pl.BlockSpec(w_uv.shape, lambda bi, i: (0, 0, 0))],
        out_specs=pl.BlockSpec((1, tq, N_ATTN_HEADS * HEAD_DIM), lambda bi, i: (bi, i, 0)),
        out_shape=jax.ShapeDtypeStruct((b, t, N_ATTN_HEADS * HEAD_DIM), BF16),
        scratch_shapes=[pltpu.VMEM((tq, t), I32),
                        pltpu.VMEM((N_ATTN_HEADS, tq, 1), F32),
                        pltpu.VMEM((N_ATTN_HEADS, tq, 1), F32),
                        pltpu.VMEM((N_ATTN_HEADS, tq, KV_LORA), F32)],
        compiler_params=_cparams("parallel", "parallel"),
        name="dsa_attention",
    )(q_idx, w_idx, q_abs, k_idx_t, c_kv, c_kv_t, w_uv)


def _pool_kernel(u_ref, halo_ref, wg_ref, sc_ref, y_ref, ext_scr, *, tm):
    i = pl.program_id(1)
    halo = halo_ref[0]
    ext_scr[0:POOL_HALO, :] = jnp.where(i > 0, halo, jnp.zeros_like(halo))
    ext_scr[POOL_HALO:POOL_HALO + tm, :] = u_ref[0]
    pos = i * tm + lax.broadcasted_iota(I32, (tm, 1), 0)
    for g, win in enumerate(POOL_WINDOWS):
        lo = g * POOL_GROUP
        tok = ext_scr[POOL_HALO:POOL_HALO + tm, lo:lo + POOL_GROUP]
        total = tok
        for k in range(1, win):
            total = total + ext_scr[POOL_HALO - k:POOL_HALO - k + tm, lo:lo + POOL_GROUP]
        cnt = jnp.minimum(pos + 1, win).astype(F32)
        mixed = (total / cnt - tok).astype(BF16)
        out = jnp.dot(mixed, wg_ref[g], preferred_element_type=F32)
        y_ref[0, :, lo:lo + POOL_GROUP] = (out * sc_ref[:, lo:lo + POOL_GROUP]).astype(y_ref.dtype)


def pool_mix(proj, w_group, scale, *, tm):
    b, t, _ = proj.shape
    halo_blocks = tm // POOL_HALO
    return pl.pallas_call(
        functools.partial(_pool_kernel, tm=tm),
        grid=(b, t // tm),
        in_specs=[pl.BlockSpec((1, tm, POOL_WIDTH), lambda bi, i: (bi, i, 0)),
                  pl.BlockSpec((1, POOL_HALO, POOL_WIDTH),
                               lambda bi, i: (bi, jnp.maximum(i * halo_blocks - 1, 0), 0)),
                  pl.BlockSpec(w_group.shape, lambda bi, i: (0, 0, 0)),
                  pl.BlockSpec((1, POOL_WIDTH), lambda bi, i: (0, 0))],
        out_specs=pl.BlockSpec((1, tm, POOL_WIDTH), lambda bi, i: (bi, i, 0)),
        out_shape=jax.ShapeDtypeStruct((b, t, POOL_WIDTH), BF16),
        scratch_shapes=[pltpu.VMEM((POOL_HALO + tm, POOL_WIDTH), F32)],
        compiler_params=_cparams("parallel", "parallel"),
        name="pool_mix",
    )(proj, proj, w_group, scale)


def _mix_out_kernel(y_ref, qm_ref, mkv_ref, wo_ref, x_ref, o_ref):
    qm = qm_ref[0].astype(BF16)
    mkv = mkv_ref[0]
    parts = [y_ref[0]]
    for hd in range(N_MEM_HEADS):
        lo = hd * HEAD_DIM
        logits = lax.dot_general(qm[:, lo:lo + HEAD_DIM], mkv[:, lo:lo + HEAD_DIM],
                                 (((1,), (1,)), ((), ())), preferred_element_type=F32) * HEAD_DIM ** -0.5
        p = jnp.exp(logits - jnp.max(logits, axis=-1, keepdims=True))
        den = jnp.sum(p, axis=-1, keepdims=True)
        att = jnp.dot(p.astype(BF16), mkv[:, MEM_WIDTH + lo:MEM_WIDTH + lo + HEAD_DIM],
                      preferred_element_type=F32)
        parts.append((att / den).astype(BF16))
    mix = jnp.concatenate(parts, axis=-1)
    o_ref[0] = x_ref[0] + jnp.dot(mix, wo_ref[...], preferred_element_type=F32)


def mix_out(y, q_mem, qm_block, mem_kv, w_out, x, *, tm):
    b, t, d = x.shape
    return pl.pallas_call(
        _mix_out_kernel,
        grid=(b, t // tm),
        in_specs=[pl.BlockSpec((1, tm, y.shape[-1]), lambda bi, i: (bi, i, 0)),
                  pl.BlockSpec((1, tm, MEM_WIDTH), lambda bi, i: (bi, i, qm_block)),
                  pl.BlockSpec((1,) + mem_kv.shape[1:], lambda bi, i: (bi, 0, 0)),
                  pl.BlockSpec(w_out.shape, lambda bi, i: (0, 0)),
                  pl.BlockSpec((1, tm, d), lambda bi, i: (bi, i, 0))],
        out_specs=pl.BlockSpec((1, tm, d), lambda bi, i: (bi, i, 0)),
        out_shape=jax.ShapeDtypeStruct((b, t, d), F32),
        compiler_params=_cparams("parallel", "parallel"),
        name="mix_out",
    )(y, q_mem, mem_kv, w_out, x)


def _pack_attn_in(w_in):
    d = w_in.shape[0]
    c_q, c_kv, k_idx, w_idx, q_mem = jnp.split(w_in, [512, 768, 832, 848], axis=-1)
    pad = jnp.zeros((d, ATTN_IN_PAD - w_in.shape[1]), w_in.dtype)
    return jnp.concatenate([c_q, c_kv, q_mem, k_idx, w_idx, pad], axis=-1)


def _pad_lanes(v, width):
    return jnp.pad(v, (0, width - v.shape[0])).reshape(1, width)


def kernel(x, mem, mixer_norm, ffn_norm, final_norm, mem_norm, w_mem_kv, w_out, w_gate, w_up, w_down,
           w_in_attn, q_norm, kv_norm, w_uq, w_uk, w_uv, w_idx_uq, idx_k_norm, idx_k_bias,
           w_in_pool, w_pool_group, pool_scale):
    b, t, d = x.shape
    depth = w_out.shape[0]
    n_mem = mem.shape[1]
    row = lambda v: v.reshape(1, -1)

    w_kv_all = jnp.transpose(w_mem_kv, (1, 0, 2)).reshape(d, depth * 2 * MEM_WIDTH).astype(BF16)
    mem_kv_all = rms_matmul(mem.reshape(b * n_mem, d), row(mem_norm), w_kv_all,
                            tm=512, tn=2 * MEM_WIDTH, out_dtype=BF16)
    mem_kv_all = mem_kv_all.reshape(b, n_mem, depth, 2 * MEM_WIDTH)

    for i in range(depth):
        j = i // 2
        mem_kv = mem_kv_all[:, :, i, :]
        w_o = w_out[i].astype(BF16)
        if i % 2 == 0:
            ckv, ckv_t, kidx_t, q_mem, q_abs, q_idx, w_idx = attn_proj(
                x, row(mixer_norm[i]), _pack_attn_in(w_in_attn[j]).astype(BF16),
                row(q_norm[j]), row(kv_norm[j]), _pad_lanes(idx_k_norm[j], 128), _pad_lanes(idx_k_bias[j], 128),
                w_uq[j].astype(BF16), jnp.transpose(w_uk[j], (0, 2, 1)).astype(BF16),
                w_idx_uq[j].astype(BF16), tm=256)
            y = dsa_attention(q_idx, w_idx, q_abs, kidx_t, ckv, ckv_t, w_uv[j].astype(BF16), tq=128, tk=256)
            x = mix_out(y, q_mem, 0, mem_kv, w_o, x, tm=512)
        else:
            proj = rms_matmul(x.reshape(b * t, d), row(mixer_norm[i]), w_in_pool[j].astype(BF16),
                              tm=512, tn=POOL_WIDTH + MEM_WIDTH, out_dtype=F32).reshape(b, t, -1)
            y = pool_mix(proj, w_pool_group[j].astype(BF16), row(pool_scale[j]), tm=512)
            x = mix_out(y, proj, POOL_WIDTH // MEM_WIDTH, mem_kv, w_o, x, tm=512)
        x = ffn(x.reshape(b * t, d), row(ffn_norm[i]), w_gate[i].astype(BF16), w_up[i].astype(BF16),
                w_down[i].astype(BF16), row(final_norm), tm=512, tf=512,
                final_norm=(i == depth - 1)).reshape(b, t, d)
    return x
```

```python
import functools

import jax
import jax.numpy as jnp
from jax import lax
from jax.experimental import pallas as pl
from jax.experimental.pallas import tpu as pltpu

BF16 = jnp.bfloat16
F32 = jnp.float32
I32 = jnp.int32

EPS = 1e-6
HEAD_DIM = 128
N_ATTN_HEADS = 12
Q_LORA = 512
KV_LORA = 256
IDX_HEADS = 16
IDX_DIM = 64
TOPK_MAX = 256
POOL_WINDOWS = (2, 4, 8, 16)
POOL_GROUP = 384
POOL_WIDTH = 1536
MEM_WIDTH = 512
N_MEM_HEADS = 4
POOL_HALO = 16

_CQ0, _CKV0, _QM0, _KI0, _WI0 = 0, 512, 768, 1280, 1344
ATTN_IN_PAD = 1408

LOG2E = 1.4426950408889634
INT_MIN = -(2 ** 31)
NEG = -1e30

VMEM_LIMIT = 56 * 1024 * 1024
DSA_TQ = 128
DSA_TK = 512


def _cparams(*sem):
    return pltpu.CompilerParams(dimension_semantics=sem, vmem_limit_bytes=VMEM_LIMIT)


def _rms(xf, g):
    ms = jnp.mean(xf * xf, axis=-1, keepdims=True)
    return xf * lax.rsqrt(ms + EPS) * g


def _rms_matmul_kernel(x_ref, g_ref, w_ref, o_ref):
    h = _rms(x_ref[...], g_ref[...]).astype(BF16)
    o_ref[...] = jnp.dot(h, w_ref[...], preferred_element_type=F32).astype(o_ref.dtype)


def rms_matmul(x, g, w, *, tm, tn, out_dtype):
    m, d = x.shape
    n = w.shape[1]
    return pl.pallas_call(
        _rms_matmul_kernel,
        grid=(m // tm, n // tn),
        in_specs=[pl.BlockSpec((tm, d), lambda i, j: (i, 0)),
                  pl.BlockSpec((1, d), lambda i, j: (0, 0)),
                  pl.BlockSpec((d, tn), lambda i, j: (0, j))],
        out_specs=pl.BlockSpec((tm, tn), lambda i, j: (i, j)),
        out_shape=jax.ShapeDtypeStruct((m, n), out_dtype),
        compiler_params=_cparams("parallel", "parallel"),
        name="rms_matmul",
    )(x, g, w)


def _ffn_kernel(x_ref, g_ref, wg_ref, wu_ref, wd_ref, fg_ref, o_ref, h_scr, *, final_norm):
    f = pl.program_id(1)

    @pl.when(f == 0)
    def _():
        x = x_ref[...]
        h_scr[...] = _rms(x, g_ref[...]).astype(BF16)
        o_ref[...] = x

    h = h_scr[...]
    gate = jnp.dot(h, wg_ref[...], preferred_element_type=F32)
    up = jnp.dot(h, wu_ref[...], preferred_element_type=F32)
    act = (gate * jax.nn.sigmoid(gate) * up).astype(BF16)
    o_ref[...] += jnp.dot(act, wd_ref[...], preferred_element_type=F32)

    if final_norm:
        @pl.when(f == pl.num_programs(1) - 1)
        def _():
            o_ref[...] = _rms(o_ref[...], fg_ref[...])


def ffn(x, g, w_gate, w_up, w_down, final_g, *, tm, tf, final_norm):
    m, d = x.shape
    dff = w_gate.shape[1]
    return pl.pallas_call(
        functools.partial(_ffn_kernel, final_norm=final_norm),
        grid=(m // tm, dff // tf),
        in_specs=[pl.BlockSpec((tm, d), lambda i, f: (i, 0)),
                  pl.BlockSpec((1, d), lambda i, f: (0, 0)),
                  pl.BlockSpec((d, tf), lambda i, f: (0, f)),
                  pl.BlockSpec((d, tf), lambda i, f: (0, f)),
                  pl.BlockSpec((tf, d), lambda i, f: (f, 0)),
                  pl.BlockSpec((1, d), lambda i, f: (0, 0))],
        out_specs=pl.BlockSpec((tm, d), lambda i, f: (i, 0)),
        out_shape=jax.ShapeDtypeStruct((m, d), F32),
        scratch_shapes=[pltpu.VMEM((tm, d), BF16)],
        compiler_params=_cparams("parallel", "arbitrary"),
        name="ffn",
    )(x, g, w_gate, w_up, w_down, final_g)


def _attn_proj_kernel(x_ref, g_ref, win_ref, qn_ref, kvn_ref, ig_ref, ib_ref, wuqt_ref, wuk_ref, wiqt_ref,
                      ckv_ref, ckvt_ref, kidx_ref, qmem_ref, qabst_ref, qidxt_ref, widxt_ref, *, tq):
    tm = x_ref.shape[1]
    h = _rms(x_ref[0], g_ref[...]).astype(BF16)
    proj = jnp.dot(h, win_ref[...], preferred_element_type=F32)
    c_q_t = _rms(proj[:, _CQ0:_CQ0 + Q_LORA], qn_ref[...]).T.astype(BF16)
    c_kv = _rms(proj[:, _CKV0:_CKV0 + KV_LORA], kvn_ref[...])
    ckv_ref[0] = c_kv.astype(BF16)
    ckvt_ref[0] = c_kv.T.astype(BF16)
    qmem_ref[0] = proj[:, _QM0:_QM0 + MEM_WIDTH].astype(BF16)
    slab = proj[:, _KI0:_KI0 + 128]
    live = lax.broadcasted_iota(I32, slab.shape, 1) < IDX_DIM
    mu = jnp.sum(jnp.where(live, slab, 0.0), axis=-1, keepdims=True) * (1.0 / IDX_DIM)
    cen = jnp.where(live, slab - mu, 0.0)
    var = jnp.sum(cen * cen, axis=-1, keepdims=True) * (1.0 / IDX_DIM)
    kin = cen * lax.rsqrt(var + EPS) * ig_ref[...] + ib_ref[...]
    kidx_ref[0] = kin[:, :IDX_DIM].astype(BF16)
    widxt_ref[0] = slab.T[IDX_DIM:IDX_DIM + IDX_HEADS, :] * (IDX_HEADS ** -0.5 * IDX_DIM ** -0.5)
    q_t = jnp.dot(wuqt_ref[...], c_q_t, preferred_element_type=F32).astype(BF16)
    for hd in range(N_ATTN_HEADS):
        qa_t = jnp.dot(wuk_ref[hd], q_t[hd * HEAD_DIM:(hd + 1) * HEAD_DIM, :], preferred_element_type=F32)
        qa_t = (qa_t * (HEAD_DIM ** -0.5 * LOG2E)).astype(BF16)
        for j in range(tm // tq):
            qabst_ref[0, j, :, hd * tq:(hd + 1) * tq] = qa_t[:, j * tq:(j + 1) * tq]
    qi_t = jnp.dot(wiqt_ref[...], c_q_t, preferred_element_type=F32).astype(BF16)
    for hd in range(IDX_HEADS):
        for j in range(tm // tq):
            qidxt_ref[0, j, :, hd * tq:(hd + 1) * tq] = qi_t[hd * IDX_DIM:(hd + 1) * IDX_DIM, j * tq:(j + 1) * tq]


def attn_proj(x, g, w_in, q_norm, kv_norm, idx_g, idx_b, w_uq_t, w_uk, w_iq_t, *, tm, tq):
    b, t, d = x.shape
    full = lambda shape: pl.BlockSpec(shape, lambda bi, i: (0,) * len(shape))
    per_tile = tm // tq
    return pl.pallas_call(
        functools.partial(_attn_proj_kernel, tq=tq),
        grid=(b, t // tm),
        in_specs=[pl.BlockSpec((1, tm, d), lambda bi, i: (bi, i, 0)),
                  full((1, d)), full(w_in.shape), full((1, Q_LORA)), full((1, KV_LORA)),
                  full((1, 128)), full((1, 128)), full(w_uq_t.shape), full(w_uk.shape), full(w_iq_t.shape)],
        out_specs=[pl.BlockSpec((1, tm, KV_LORA), lambda bi, i: (bi, i, 0)),
                   pl.BlockSpec((1, KV_LORA, tm), lambda bi, i: (bi, 0, i)),
                   pl.BlockSpec((1, tm, IDX_DIM), lambda bi, i: (bi, i, 0)),
                   pl.BlockSpec((1, tm, MEM_WIDTH), lambda bi, i: (bi, i, 0)),
                   pl.BlockSpec((1, per_tile, KV_LORA, N_ATTN_HEADS * tq), lambda bi, i: (bi, i, 0, 0)),
                   pl.BlockSpec((1, per_tile, IDX_DIM, IDX_HEADS * tq), lambda bi, i: (bi, i, 0, 0)),
                   pl.BlockSpec((1, IDX_HEADS, tm), lambda bi, i: (bi, 0, i))],
        out_shape=[jax.ShapeDtypeStruct((b, t, KV_LORA), BF16),
                   jax.ShapeDtypeStruct((b, KV_LORA, t), BF16),
                   jax.ShapeDtypeStruct((b, t, IDX_DIM), BF16),
                   jax.ShapeDtypeStruct((b, t, MEM_WIDTH), BF16),
                   jax.ShapeDtypeStruct((b, t // tq, KV_LORA, N_ATTN_HEADS * tq), BF16),
                   jax.ShapeDtypeStruct((b, t // tq, IDX_DIM, IDX_HEADS * tq), BF16),
                   jax.ShapeDtypeStruct((b, IDX_HEADS, t), F32)],
        compiler_params=_cparams("parallel", "parallel"),
        name="attn_proj",
    )(x, g, w_in, q_norm, kv_norm, idx_g, idx_b, w_uq_t, w_uk, w_iq_t)


def _dsa_kernel(qidxt_ref, widxt_ref, qabst_ref, kidx_ref, ckv_ref, ckvt_ref, wuv_ref, y_ref,
                key_scr, m_scr, l_scr, acc_scr, *, tq, tk, k_top):
    i = pl.program_id(1)
    q0 = i * tq
    n_chunks = (q0 + tq + tk - 1) // tk
    qpos = q0 + lax.broadcasted_iota(I32, (1, tq), 1)

    w = widxt_ref[0]

    def score_body(c, carry):
        k0 = pl.multiple_of(c * tk, tk)
        kc = kidx_ref[0, pl.ds(k0, tk), :]
        score = None
        for j in range(IDX_HEADS // 2):
            s = jnp.dot(kc, qidxt_ref[0, 0, :, j * 2 * tq:(j + 1) * 2 * tq], preferred_element_type=F32)
            for u in range(2):
                hd = 2 * j + u
                term = jnp.maximum(s[:, u * tq:(u + 1) * tq], 0.0) * w[hd:hd + 1, :]
                score = term if score is None else score + term
        bits = pltpu.bitcast(score, I32)
        key = bits ^ ((bits >> 31) & 0x7FFFFFFF)
        kpos = k0 + lax.broadcasted_iota(I32, (tk, 1), 0)
        key_scr[pl.ds(k0, tk), :] = jnp.where(kpos <= qpos, key, INT_MIN)
        return carry

    lax.fori_loop(0, n_chunks, score_body, 0)

    def count_ge(cand):
        def body(c, acc):
            k0 = pl.multiple_of(c * tk, tk)
            hit = jnp.where(key_scr[pl.ds(k0, tk), :] >= cand, 1, 0)
            return acc + jnp.sum(hit.reshape(tk // 8, 8, tq), axis=0)
        acc = lax.fori_loop(0, n_chunks, body, jnp.zeros((8, tq), I32))
        return jnp.sum(acc, axis=0, keepdims=True)

    def bit_body(b, thr):
        cand = thr + lax.shift_left(jnp.int32(1), 31 - b)
        return jnp.where(count_ge(cand) >= k_top, cand, thr)

    thr = lax.fori_loop(0, 32, bit_body, jnp.full((1, tq), INT_MIN, I32))
    thr = jnp.maximum(thr, INT_MIN + 1)

    m_scr[...] = jnp.full(m_scr.shape, NEG, F32)
    l_scr[...] = jnp.zeros(l_scr.shape, F32)
    acc_scr[...] = jnp.zeros(acc_scr.shape, F32)

    def att_body(c, carry):
        k0 = pl.multiple_of(c * tk, tk)
        kv = ckv_ref[0, pl.ds(k0, tk), :]
        kv_t = ckvt_ref[0, :, pl.ds(k0, tk)]
        sel = key_scr[pl.ds(k0, tk), :] >= thr
        for j in range(N_ATTN_HEADS // 2):
            cols = slice(j * 2 * tq, (j + 1) * 2 * tq)
            lt = jnp.dot(kv, qabst_ref[0, 0, :, cols], preferred_element_type=F32)
            ps, alphas = [], []
            for u in range(2):
                hc = slice((2 * j + u) * tq, (2 * j + u + 1) * tq)
                lg = jnp.where(sel, lt[:, u * tq:(u + 1) * tq], NEG)
                m_prev = m_scr[:, hc]
                m_new = jnp.maximum(m_prev, jnp.max(lg, axis=0, keepdims=True))
                alpha = jnp.exp2(m_prev - m_new)
                p = jnp.exp2(lg - m_new)
                l_scr[:, hc] = alpha * l_scr[:, hc] + jnp.sum(p, axis=0, keepdims=True)
                m_scr[:, hc] = m_new
                ps.append(p.astype(BF16))
                alphas.append(alpha)
            pv = jnp.dot(kv_t, jnp.concatenate(ps, axis=1), preferred_element_type=F32)
            acc_scr[:, cols] = jnp.concatenate(alphas, axis=1) * acc_scr[:, cols] + pv
        return carry

    lax.fori_loop(0, n_chunks, att_body, 0)

    for hd in range(N_ATTN_HEADS):
        hc = slice(hd * tq, (hd + 1) * tq)
        o_lat = (acc_scr[:, hc] * (1.0 / l_scr[:, hc])).T.astype(BF16)
        y_ref[0, :, hd * HEAD_DIM:(hd + 1) * HEAD_DIM] = jnp.dot(
            o_lat, wuv_ref[hd], preferred_element_type=F32).astype(y_ref.dtype)


def dsa_attention(q_idx_t, w_idx_t, q_abs_t, k_idx, c_kv, c_kv_t, w_uv, *, tq, tk):
    b, t, _ = c_kv.shape
    k_top = min(TOPK_MAX, t // 4)
    return pl.pallas_call(
        functools.partial(_dsa_kernel, tq=tq, tk=tk, k_top=k_top),
        grid=(b, t // tq),
        in_specs=[pl.BlockSpec((1, 1, IDX_DIM, IDX_HEADS * tq), lambda bi, i: (bi, i, 0, 0)),
                  pl.BlockSpec((1, IDX_HEADS, tq), lambda bi, i: (bi, 0, i)),
                  pl.BlockSpec((1, 1, KV_LORA, N_ATTN_HEADS * tq), lambda bi, i: (bi, i, 0, 0)),
                  pl.BlockSpec((1, t, IDX_DIM), lambda bi, i: (bi, 0, 0)),
                  pl.BlockSpec((1, t, KV_LORA), lambda bi, i: (bi, 0, 0)),
                  pl.BlockSpec((1, KV_LORA, t), lambda bi, i: (bi, 0, 0)),
                  pl.BlockSpec(w_uv.shape, lambda bi, i: (0, 0, 0))],
        out_specs=pl.BlockSpec((1, tq, N_ATTN_HEADS * HEAD_DIM), lambda bi, i: (bi, i, 0)),
        out_shape=jax.ShapeDtypeStruct((b, t, N_ATTN_HEADS * HEAD_DIM), BF16),
        scratch_shapes=[pltpu.VMEM((t, tq), I32),
                        pltpu.VMEM((1, N_ATTN_HEADS * tq), F32),
                        pltpu.VMEM((1, N_ATTN_HEADS * tq), F32),
                        pltpu.VMEM((KV_LORA, N_ATTN_HEADS * tq), F32)],
        compiler_params=_cparams("parallel", "parallel"),
        name="dsa_attention",
    )(q_idx_t, w_idx_t, q_abs_t, k_idx, c_kv, c_kv_t, w_uv)


def _pool_kernel(u_ref, halo_ref, wg_ref, sc_ref, y_ref, ext_scr, *, tm):
    i = pl.program_id(1)
    halo = halo_ref[0]
    ext_scr[0:POOL_HALO, :] = jnp.where(i > 0, halo, jnp.zeros_like(halo))
    ext_scr[POOL_HALO:POOL_HALO + tm, :] = u_ref[0]
    pos = i * tm + lax.broadcasted_iota(I32, (tm, 1), 0)
    for g, win in enumerate(POOL_WINDOWS):
        lo = g * POOL_GROUP
        tok = ext_scr[POOL_HALO:POOL_HALO + tm, lo:lo + POOL_GROUP]
        total = tok
        for k in range(1, win):
            total = total + ext_scr[POOL_HALO - k:POOL_HALO - k + tm, lo:lo + POOL_GROUP]
        cnt = jnp.minimum(pos + 1, win).astype(F32)
        mixed = (total / cnt - tok).astype(BF16)
        out = jnp.dot(mixed, wg_ref[g], preferred_element_type=F32)
        y_ref[0, :, lo:lo + POOL_GROUP] = (out * sc_ref[:, lo:lo + POOL_GROUP]).astype(y_ref.dtype)


def pool_mix(proj, w_group, scale, *, tm):
    b, t, _ = proj.shape
    halo_blocks = tm // POOL_HALO
    return pl.pallas_call(
        functools.partial(_pool_kernel, tm=tm),
        grid=(b, t // tm),
        in_specs=[pl.BlockSpec((1, tm, POOL_WIDTH), lambda bi, i: (bi, i, 0)),
                  pl.BlockSpec((1, POOL_HALO, POOL_WIDTH),
                               lambda bi, i: (bi, jnp.maximum(i * halo_blocks - 1, 0), 0)),
                  pl.BlockSpec(w_group.shape, lambda bi, i: (0, 0, 0)),
                  pl.BlockSpec((1, POOL_WIDTH), lambda bi, i: (0, 0))],
        out_specs=pl.BlockSpec((1, tm, POOL_WIDTH), lambda bi, i: (bi, i, 0)),
        out_shape=jax.ShapeDtypeStruct((b, t, POOL_WIDTH), BF16),
        scratch_shapes=[pltpu.VMEM((POOL_HALO + tm, POOL_WIDTH), F32)],
        compiler_params=_cparams("parallel", "parallel"),
        name="pool_mix",
    )(proj, proj, w_group, scale)


def _mix_out_kernel(y_ref, qm_ref, mkv_ref, wo_ref, x_ref, o_ref):
    qm = qm_ref[0].astype(BF16)
    mkv = mkv_ref[0]
    parts = [y_ref[0]]
    for hd in range(N_MEM_HEADS):
        lo = hd * HEAD_DIM
        logits = lax.dot_general(qm[:, lo:lo + HEAD_DIM], mkv[:, lo:lo + HEAD_DIM],
                                 (((1,), (1,)), ((), ())), preferred_element_type=F32) * HEAD_DIM ** -0.5
        p = jnp.exp(logits - jnp.max(logits, axis=-1, keepdims=True))
        den = jnp.sum(p, axis=-1, keepdims=True)
        att = jnp.dot(p.astype(BF16), mkv[:, MEM_WIDTH + lo:MEM_WIDTH + lo + HEAD_DIM],
                      preferred_element_type=F32)
        parts.append((att / den).astype(BF16))
    mix = jnp.concatenate(parts, axis=-1)
    o_ref[0] = x_ref[0] + jnp.dot(mix, wo_ref[...], preferred_element_type=F32)


def mix_out(y, q_mem, qm_block, mem_kv, w_out, x, *, tm):
    b, t, d = x.shape
    return pl.pallas_call(
        _mix_out_kernel,
        grid=(b, t // tm),
        in_specs=[pl.BlockSpec((1, tm, y.shape[-1]), lambda bi, i: (bi, i, 0)),
                  pl.BlockSpec((1, tm, MEM_WIDTH), lambda bi, i: (bi, i, qm_block)),
                  pl.BlockSpec((1,) + mem_kv.shape[1:], lambda bi, i: (bi, 0, 0)),
                  pl.BlockSpec(w_out.shape, lambda bi, i: (0, 0)),
                  pl.BlockSpec((1, tm, d), lambda bi, i: (bi, i, 0))],
        out_specs=pl.BlockSpec((1, tm, d), lambda bi, i: (bi, i, 0)),
        out_shape=jax.ShapeDtypeStruct((b, t, d), F32),
        compiler_params=_cparams("parallel", "parallel"),
        name="mix_out",
    )(y, q_mem, mem_kv, w_out, x)


def _pack_attn_in(w_in):
    d = w_in.shape[0]
    c_q, c_kv, k_idx, w_idx, q_mem = jnp.split(w_in, [512, 768, 832, 848], axis=-1)
    pad = jnp.zeros((d, ATTN_IN_PAD - w_in.shape[1]), w_in.dtype)
    return jnp.concatenate([c_q, c_kv, q_mem, k_idx, w_idx, pad], axis=-1)


def _pad_lanes(v, width):
    return jnp.pad(v, (0, width - v.shape[0])).reshape(1, width)


def kernel(x, mem, mixer_norm, ffn_norm, final_norm, mem_norm, w_mem_kv, w_out, w_gate, w_up, w_down,
           w_in_attn, q_norm, kv_norm, w_uq, w_uk, w_uv, w_idx_uq, idx_k_norm, idx_k_bias,
           w_in_pool, w_pool_group, pool_scale):
    b, t, d = x.shape
    depth = w_out.shape[0]
    n_mem = mem.shape[1]
    row = lambda v: v.reshape(1, -1)

    w_kv_all = jnp.transpose(w_mem_kv, (1, 0, 2)).reshape(d, depth * 2 * MEM_WIDTH).astype(BF16)
    mem_kv_all = rms_matmul(mem.reshape(b * n_mem, d), row(mem_norm), w_kv_all,
                            tm=512, tn=2 * MEM_WIDTH, out_dtype=BF16)
    mem_kv_all = mem_kv_all.reshape(b, n_mem, depth, 2 * MEM_WIDTH)

    for i in range(depth):
        j = i // 2
        mem_kv = mem_kv_all[:, :, i, :]
        w_o = w_out[i].astype(BF16)
        if i % 2 == 0:
            ckv, ckv_t, kidx, q_mem, q_abs_t, q_idx_t, w_idx_t = attn_proj(
                x, row(mixer_norm[i]), _pack_attn_in(w_in_attn[j]).astype(BF16),
                row(q_norm[j]), row(kv_norm[j]), _pad_lanes(idx_k_norm[j], 128), _pad_lanes(idx_k_bias[j], 128),
                w_uq[j].T.astype(BF16), w_uk[j].astype(BF16), w_idx_uq[j].T.astype(BF16),
                tm=2 * DSA_TQ, tq=DSA_TQ)
            y = dsa_attention(q_idx_t, w_idx_t, q_abs_t, kidx, ckv, ckv_t, w_uv[j].astype(BF16),
                              tq=DSA_TQ, tk=DSA_TK)
            x = mix_out(y, q_mem, 0, mem_kv, w_o, x, tm=512)
        else:
            proj = rms_matmul(x.reshape(b * t, d), row(mixer_norm[i]), w_in_pool[j].astype(BF16),
                              tm=512, tn=POOL_WIDTH + MEM_WIDTH, out_dtype=F32).reshape(b, t, -1)
            y = pool_mix(proj, w_pool_group[j].astype(BF16), row(pool_scale[j]), tm=512)
            x = mix_out(y, proj, POOL_WIDTH // MEM_WIDTH, mem_kv, w_o, x, tm=512)
        x = ffn(x.reshape(b * t, d), row(ffn_norm[i]), w_gate[i].astype(BF16), w_up[i].astype(BF16),
                w_down[i].astype(BF16), row(final_norm), tm=512, tf=512,
                final_norm=(i == depth - 1)).reshape(b, t, d)
    return x
```

```python
import functools

import jax
import jax.numpy as jnp
from jax import lax
from jax.experimental import pallas as pl
from jax.experimental.pallas import tpu as pltpu

BF16 = jnp.bfloat16
F32 = jnp.float32
I32 = jnp.int32

EPS = 1e-6
HEAD_DIM = 128
N_ATTN_HEADS = 12
Q_LORA = 512
KV_LORA = 256
IDX_HEADS = 16
IDX_DIM = 64
TOPK_MAX = 256
POOL_WINDOWS = (2, 4, 8, 16)
POOL_GROUP = 384
POOL_WIDTH = 1536
MEM_WIDTH = 512
N_MEM_HEADS = 4
POOL_HALO = 16

_CQ0, _CKV0, _QM0, _KI0, _WI0 = 0, 512, 768, 1280, 1344
ATTN_IN_PAD = 1408

LOG2E = 1.4426950408889634
INT_MIN = -(2 ** 31)
INT_MAX = 2 ** 31 - 1
ONES_ROWS = 16
NEG = -1e30

VMEM_LIMIT = 56 * 1024 * 1024
DSA_TQ = 128
DSA_TK = 512
FFN_TM, FFN_TF = 1024, 512
MIX_TM = 512


def _cparams(*sem):
    return pltpu.CompilerParams(dimension_semantics=sem, vmem_limit_bytes=VMEM_LIMIT)


def _rms(xf, g):
    ms = jnp.mean(xf * xf, axis=-1, keepdims=True)
    return xf * lax.rsqrt(ms + EPS) * g


def _rms_matmul_kernel(x_ref, g_ref, w_ref, o_ref):
    h = _rms(x_ref[...], g_ref[...]).astype(BF16)
    o_ref[...] = jnp.dot(h, w_ref[...], preferred_element_type=F32).astype(o_ref.dtype)


def rms_matmul(x, g, w, layer, *, tm, out_dtype):
    m, d = x.shape
    n = w.shape[2]
    return pl.pallas_call(
        _rms_matmul_kernel,
        grid=(m // tm,),
        in_specs=[pl.BlockSpec((tm, d), lambda i: (i, 0)),
                  pl.BlockSpec((1, d), lambda i: (0, 0)),
                  pl.BlockSpec((None, d, n), lambda i: (layer, 0, 0))],
        out_specs=pl.BlockSpec((tm, n), lambda i: (i, 0)),
        out_shape=jax.ShapeDtypeStruct((m, n), out_dtype),
        compiler_params=_cparams("parallel"),
        name="rms_matmul",
    )(x, g, w)


def rms_matmul_layers(x, g, w, *, tm, out_dtype):
    m, d = x.shape
    n_layers, _, n = w.shape
    return pl.pallas_call(
        _rms_matmul_kernel,
        grid=(m // tm, n_layers),
        in_specs=[pl.BlockSpec((tm, d), lambda i, l: (i, 0)),
                  pl.BlockSpec((1, d), lambda i, l: (0, 0)),
                  pl.BlockSpec((None, d, n), lambda i, l: (l, 0, 0))],
        out_specs=pl.BlockSpec((None, tm, n), lambda i, l: (l, i, 0)),
        out_shape=jax.ShapeDtypeStruct((n_layers, m, n), out_dtype),
        compiler_params=_cparams("parallel", "parallel"),
        name="rms_matmul_layers",
    )(x, g, w)


def _ffn_kernel(x_ref, g_ref, wg_ref, wu_ref, wd_ref, fg_ref, o_ref, h_scr, *, final_norm):
    f = pl.program_id(1)

    @pl.when(f == 0)
    def _():
        x = x_ref[...]
        h_scr[...] = _rms(x, g_ref[...]).astype(BF16)
        o_ref[...] = x

    h = h_scr[...]
    gate = jnp.dot(h, wg_ref[...], preferred_element_type=F32)
    up = jnp.dot(h, wu_ref[...], preferred_element_type=F32)
    act = (gate * jax.nn.sigmoid(gate) * up).astype(BF16)
    o_ref[...] += jnp.dot(act, wd_ref[...], preferred_element_type=F32)

    if final_norm:
        @pl.when(f == pl.num_programs(1) - 1)
        def _():
            o_ref[...] = _rms(o_ref[...], fg_ref[...])


def ffn(x, g, w_gate, w_up, w_down, layer, final_g, *, tm, tf, final_norm):
    m, d = x.shape
    dff = w_gate.shape[2]
    return pl.pallas_call(
        functools.partial(_ffn_kernel, final_norm=final_norm),
        grid=(m // tm, dff // tf),
        in_specs=[pl.BlockSpec((tm, d), lambda i, f: (i, 0), pipeline_mode=pl.Buffered(1)),
                  pl.BlockSpec((1, d), lambda i, f: (0, 0)),
                  pl.BlockSpec((None, d, tf), lambda i, f: (layer, 0, f)),
                  pl.BlockSpec((None, d, tf), lambda i, f: (layer, 0, f)),
                  pl.BlockSpec((None, tf, d), lambda i, f: (layer, f, 0)),
                  pl.BlockSpec((1, d), lambda i, f: (0, 0))],
        out_specs=pl.BlockSpec((tm, d), lambda i, f: (i, 0)),
        out_shape=jax.ShapeDtypeStruct((m, d), F32),
        scratch_shapes=[pltpu.VMEM((tm, d), BF16)],
        compiler_params=_cparams("parallel", "arbitrary"),
        name="ffn",
    )(x, g, w_gate, w_up, w_down, final_g)


def _attn_proj_kernel(x_ref, g_ref, win_ref, qn_ref, kvn_ref, ig_ref, ib_ref, wuqt_ref, wuk_ref, wiqt_ref,
                      ckv_ref, ckvt_ref, kidx_ref, qmem_ref, qabst_ref, qidxt_ref, widxt_ref, *, tq):
    tm = x_ref.shape[1]
    h = _rms(x_ref[0], g_ref[...]).astype(BF16)
    proj = jnp.dot(h, win_ref[...], preferred_element_type=F32)
    c_q_t = _rms(proj[:, _CQ0:_CQ0 + Q_LORA], qn_ref[...]).T.astype(BF16)
    c_kv = _rms(proj[:, _CKV0:_CKV0 + KV_LORA], kvn_ref[...])
    ckv_ref[0] = c_kv.astype(BF16)
    ckvt_ref[0, 0:KV_LORA, :] = c_kv.T.astype(BF16)
    ckvt_ref[0, KV_LORA:KV_LORA + ONES_ROWS, :] = jnp.ones((ONES_ROWS, tm), BF16)
    qmem_ref[0] = proj[:, _QM0:_QM0 + MEM_WIDTH].astype(BF16)
    slab = proj[:, _KI0:_KI0 + 128]
    live = lax.broadcasted_iota(I32, slab.shape, 1) < IDX_DIM
    mu = jnp.sum(jnp.where(live, slab, 0.0), axis=-1, keepdims=True) * (1.0 / IDX_DIM)
    cen = jnp.where(live, slab - mu, 0.0)
    var = jnp.sum(cen * cen, axis=-1, keepdims=True) * (1.0 / IDX_DIM)
    kin = cen * lax.rsqrt(var + EPS) * ig_ref[...] + ib_ref[...]
    kidx_ref[0] = kin[:, :IDX_DIM].astype(BF16)
    widxt_ref[0] = slab.T[IDX_DIM:IDX_DIM + IDX_HEADS, :] * (IDX_HEADS ** -0.5 * IDX_DIM ** -0.5)
    q_t = jnp.dot(wuqt_ref[...], c_q_t, preferred_element_type=F32).astype(BF16)
    for hd in range(N_ATTN_HEADS):
        qa_t = jnp.dot(wuk_ref[hd], q_t[hd * HEAD_DIM:(hd + 1) * HEAD_DIM, :], preferred_element_type=F32)
        qa_t = (qa_t * (HEAD_DIM ** -0.5 * LOG2E)).astype(BF16)
        for j in range(tm // tq):
            qabst_ref[0, j, :, hd * tq:(hd + 1) * tq] = qa_t[:, j * tq:(j + 1) * tq]
    qi_t = jnp.dot(wiqt_ref[...], c_q_t, preferred_element_type=F32).astype(BF16)
    for hd in range(IDX_HEADS):
        for j in range(tm // tq):
            qidxt_ref[0, j, :, hd * tq:(hd + 1) * tq] = qi_t[hd * IDX_DIM:(hd + 1) * IDX_DIM, j * tq:(j + 1) * tq]


def attn_proj(x, g, w_in, q_norm, kv_norm, idx_g, idx_b, w_uq_t, w_uk, w_iq_t, layer, *, tm, tq):
    b, t, d = x.shape
    full = lambda shape: pl.BlockSpec(shape, lambda bi, i: (0,) * len(shape))
    stacked = lambda w: pl.BlockSpec((None,) + w.shape[1:], lambda bi, i: (layer,) + (0,) * (w.ndim - 1))
    per_tile = tm // tq
    return pl.pallas_call(
        functools.partial(_attn_proj_kernel, tq=tq),
        grid=(b, t // tm),
        in_specs=[pl.BlockSpec((1, tm, d), lambda bi, i: (bi, i, 0)),
                  full((1, d)), stacked(w_in), full((1, Q_LORA)), full((1, KV_LORA)),
                  full((1, 128)), full((1, 128)), stacked(w_uq_t), stacked(w_uk), stacked(w_iq_t)],
        out_specs=[pl.BlockSpec((1, tm, KV_LORA), lambda bi, i: (bi, i, 0)),
                   pl.BlockSpec((1, KV_LORA + ONES_ROWS, tm), lambda bi, i: (bi, 0, i)),
                   pl.BlockSpec((1, tm, IDX_DIM), lambda bi, i: (bi, i, 0)),
                   pl.BlockSpec((1, tm, MEM_WIDTH), lambda bi, i: (bi, i, 0)),
                   pl.BlockSpec((1, per_tile, KV_LORA, N_ATTN_HEADS * tq), lambda bi, i: (bi, i, 0, 0)),
                   pl.BlockSpec((1, per_tile, IDX_DIM, IDX_HEADS * tq), lambda bi, i: (bi, i, 0, 0)),
                   pl.BlockSpec((1, IDX_HEADS, tm), lambda bi, i: (bi, 0, i))],
        out_shape=[jax.ShapeDtypeStruct((b, t, KV_LORA), BF16),
                   jax.ShapeDtypeStruct((b, KV_LORA + ONES_ROWS, t), BF16),
                   jax.ShapeDtypeStruct((b, t, IDX_DIM), BF16),
                   jax.ShapeDtypeStruct((b, t, MEM_WIDTH), BF16),
                   jax.ShapeDtypeStruct((b, t // tq, KV_LORA, N_ATTN_HEADS * tq), BF16),
                   jax.ShapeDtypeStruct((b, t // tq, IDX_DIM, IDX_HEADS * tq), BF16),
                   jax.ShapeDtypeStruct((b, IDX_HEADS, t), F32)],
        compiler_params=_cparams("parallel", "parallel"),
        name="attn_proj",
    )(x, g, w_in, q_norm, kv_norm, idx_g, idx_b, w_uq_t, w_uk, w_iq_t)


def _dsa_kernel(qidxt_ref, widxt_ref, qabst_ref, kidx_ref, ckv_ref, ckvt_ref, wuv_ref, y_ref,
                key_scr, m_scr, acc_scr, *, tq, tk, k_top):
    i = pl.program_id(1)
    q0 = i * tq
    n_chunks = (q0 + tq + tk - 1) // tk
    qpos = q0 + lax.broadcasted_iota(I32, (1, tq), 1)

    w = widxt_ref[0]

    def score_body(c, carry):
        k0 = pl.multiple_of(c * tk, tk)
        kc = kidx_ref[0, pl.ds(k0, tk), :]
        score = None
        for j in range(IDX_HEADS // 2):
            s = jnp.dot(kc, qidxt_ref[0, 0, :, j * 2 * tq:(j + 1) * 2 * tq], preferred_element_type=F32)
            for u in range(2):
                hd = 2 * j + u
                term = jnp.maximum(s[:, u * tq:(u + 1) * tq], 0.0) * w[hd:hd + 1, :]
                score = term if score is None else score + term
        bits = pltpu.bitcast(score, I32)
        key = bits ^ ((bits >> 31) & 0x7FFFFFFF)
        kpos = k0 + lax.broadcasted_iota(I32, (tk, 1), 0)
        key_scr[pl.ds(k0, tk), :] = jnp.where(kpos <= qpos, key, INT_MIN)
        return carry

    lax.fori_loop(0, n_chunks, score_body, 0)

    def count_rows(hit_fn):
        def body(c, acc):
            k0 = pl.multiple_of(c * tk, tk)
            kpos = k0 + lax.broadcasted_iota(I32, (tk, 1), 0)
            hit = jnp.where(hit_fn(key_scr[pl.ds(k0, tk), :], kpos), 1, 0)
            return acc + jnp.sum(hit.reshape(tk // 8, 8, tq), axis=0)
        acc = lax.fori_loop(0, n_chunks, body, jnp.zeros((8, tq), I32))
        return jnp.sum(acc, axis=0, keepdims=True)

    def count_ge(cand):
        return count_rows(lambda kk, kpos: kk >= cand)

    def search_body(b, state):
        thr, cnt = state
        cand = thr + lax.shift_left(jnp.int32(1), 31 - b)
        c = count_ge(cand)
        take = c >= k_top
        return jnp.where(take, cand, thr), jnp.where(take, c, cnt)

    thr, cnt = lax.fori_loop(0, 32, search_body,
                             (jnp.full((1, tq), INT_MIN, I32), jnp.full((1, tq), -1, I32)))

    @pl.when(jnp.sum(jnp.where(cnt > k_top, 1, 0)) > 0)
    def _():
        at_max = thr == INT_MAX
        above = jnp.where(at_max, 0, count_ge(jnp.where(at_max, thr, thr + 1)))
        need = k_top - above
        pos_bits = max(1, (key_scr.shape[0] - 1).bit_length())

        def pos_body(b, lo):
            cand = lo + lax.shift_left(jnp.int32(1), pos_bits - 1 - b)
            before = count_rows(lambda kk, kpos: (kk == thr) & (kpos < cand))
            return jnp.where(before < need, cand, lo)

        last = lax.fori_loop(0, pos_bits, pos_body, jnp.zeros((1, tq), I32))

        def drop_body(c, carry):
            k0 = pl.multiple_of(c * tk, tk)
            kpos = k0 + lax.broadcasted_iota(I32, (tk, 1), 0)
            kk = key_scr[pl.ds(k0, tk), :]
            key_scr[pl.ds(k0, tk), :] = jnp.where((kk == thr) & (kpos > last), INT_MIN, kk)
            return carry

        lax.fori_loop(0, n_chunks, drop_body, 0)

    thr = jnp.maximum(thr, INT_MIN + 1)

    m_scr[...] = jnp.full(m_scr.shape, NEG, F32)
    acc_scr[...] = jnp.zeros(acc_scr.shape, F32)

    def att_body(c, carry):
        k0 = pl.multiple_of(c * tk, tk)
        kv = ckv_ref[0, pl.ds(k0, tk), :]
        kv_t = ckvt_ref[0, :, pl.ds(k0, tk)]
        bias = jnp.where(key_scr[pl.ds(k0, tk), :] >= thr, 0.0, NEG)
        for j in range(N_ATTN_HEADS // 2):
            cols = slice(j * 2 * tq, (j + 1) * 2 * tq)
            lt = jnp.dot(kv, qabst_ref[0, 0, :, cols], preferred_element_type=F32)
            ps, alphas = [], []
            for u in range(2):
                hc = slice((2 * j + u) * tq, (2 * j + u + 1) * tq)
                lg = lt[:, u * tq:(u + 1) * tq] + bias
                m_prev = m_scr[:, hc]
                m_new = jnp.maximum(m_prev, jnp.max(lg, axis=0, keepdims=True))
                ps.append(jnp.exp2(lg - m_new).astype(BF16))
                alphas.append(jnp.exp2(m_prev - m_new))
                m_scr[:, hc] = m_new
            pv = jnp.dot(kv_t, jnp.concatenate(ps, axis=1), preferred_element_type=F32)
            acc_scr[:, cols] = jnp.concatenate(alphas, axis=1) * acc_scr[:, cols] + pv
        return carry

    lax.fori_loop(0, n_chunks, att_body, 0)

    for hd in range(N_ATTN_HEADS):
        hc = slice(hd * tq, (hd + 1) * tq)
        den = acc_scr[KV_LORA:KV_LORA + 1, hc]
        o_lat = (acc_scr[0:KV_LORA, hc] * (1.0 / den)).T.astype(BF16)
        y_ref[0, :, hd * HEAD_DIM:(hd + 1) * HEAD_DIM] = jnp.dot(
            o_lat, wuv_ref[hd], preferred_element_type=F32).astype(y_ref.dtype)


def dsa_attention(q_idx_t, w_idx_t, q_abs_t, k_idx, c_kv, c_kv_t, w_uv, layer, *, tq, tk):
    b, t, _ = c_kv.shape
    k_top = min(TOPK_MAX, t // 4)
    return pl.pallas_call(
        functools.partial(_dsa_kernel, tq=tq, tk=tk, k_top=k_top),
        grid=(b, t // tq),
        in_specs=[pl.BlockSpec((1, 1, IDX_DIM, IDX_HEADS * tq), lambda bi, i: (bi, i, 0, 0)),
                  pl.BlockSpec((1, IDX_HEADS, tq), lambda bi, i: (bi, 0, i)),
                  pl.BlockSpec((1, 1, KV_LORA, N_ATTN_HEADS * tq), lambda bi, i: (bi, i, 0, 0)),
                  pl.BlockSpec((1, t, IDX_DIM), lambda bi, i: (bi, 0, 0)),
                  pl.BlockSpec((1, t, KV_LORA), lambda bi, i: (bi, 0, 0)),
                  pl.BlockSpec((1, KV_LORA + ONES_ROWS, t), lambda bi, i: (bi, 0, 0)),
                  pl.BlockSpec((None,) + w_uv.shape[1:], lambda bi, i: (layer, 0, 0, 0))],
        out_specs=pl.BlockSpec((1, tq, N_ATTN_HEADS * HEAD_DIM), lambda bi, i: (bi, i, 0)),
        out_shape=jax.ShapeDtypeStruct((b, t, N_ATTN_HEADS * HEAD_DIM), BF16),
        scratch_shapes=[pltpu.VMEM((t, tq), I32),
                        pltpu.VMEM((1, N_ATTN_HEADS * tq), F32),
                        pltpu.VMEM((KV_LORA + ONES_ROWS, N_ATTN_HEADS * tq), F32)],
        compiler_params=_cparams("parallel", "parallel"),
        name="dsa_attention",
    )(q_idx_t, w_idx_t, q_abs_t, k_idx, c_kv, c_kv_t, w_uv)


def _pool_kernel(u_ref, halo_ref, wg_ref, sc_ref, y_ref, ext_scr, *, tm):
    i = pl.program_id(1)
    halo = halo_ref[0]
    ext_scr[0:POOL_HALO, :] = jnp.where(i > 0, halo, jnp.zeros_like(halo))
    ext_scr[POOL_HALO:POOL_HALO + tm, :] = u_ref[0]
    pos = i * tm + lax.broadcasted_iota(I32, (tm, 1), 0)
    for g, win in enumerate(POOL_WINDOWS):
        lo = g * POOL_GROUP
        tok = ext_scr[POOL_HALO:POOL_HALO + tm, lo:lo + POOL_GROUP]
        total = tok
        for k in range(1, win):
            total = total + ext_scr[POOL_HALO - k:POOL_HALO - k + tm, lo:lo + POOL_GROUP]
        cnt = jnp.minimum(pos + 1, win).astype(F32)
        mixed = (total / cnt - tok).astype(BF16)
        out = jnp.dot(mixed, wg_ref[g], preferred_element_type=F32)
        y_ref[0, :, lo:lo + POOL_GROUP] = (out * sc_ref[:, lo:lo + POOL_GROUP]).astype(y_ref.dtype)


def pool_mix(proj, w_group, scale, layer, *, tm):
    b, t, _ = proj.shape
    halo_blocks = tm // POOL_HALO
    return pl.pallas_call(
        functools.partial(_pool_kernel, tm=tm),
        grid=(b, t // tm),
        in_specs=[pl.BlockSpec((1, tm, POOL_WIDTH), lambda bi, i: (bi, i, 0)),
                  pl.BlockSpec((1, POOL_HALO, POOL_WIDTH),
                               lambda bi, i: (bi, jnp.maximum(i * halo_blocks - 1, 0), 0)),
                  pl.BlockSpec((None,) + w_group.shape[1:], lambda bi, i: (layer, 0, 0, 0)),
                  pl.BlockSpec((1, POOL_WIDTH), lambda bi, i: (0, 0))],
        out_specs=pl.BlockSpec((1, tm, POOL_WIDTH), lambda bi, i: (bi, i, 0)),
        out_shape=jax.ShapeDtypeStruct((b, t, POOL_WIDTH), BF16),
        scratch_shapes=[pltpu.VMEM((POOL_HALO + tm, POOL_WIDTH), F32)],
        compiler_params=_cparams("parallel", "parallel"),
        name="pool_mix",
    )(proj, proj, w_group, scale)


def _mix_out_kernel(y_ref, qm_ref, mkv_ref, wo_ref, x_ref, o_ref):
    qm = qm_ref[0].astype(BF16)
    mkv = mkv_ref[0]
    parts = [y_ref[0]]
    for hd in range(N_MEM_HEADS):
        lo = hd * HEAD_DIM
        logits = lax.dot_general(qm[:, lo:lo + HEAD_DIM], mkv[:, lo:lo + HEAD_DIM],
                                 (((1,), (1,)), ((), ())), preferred_element_type=F32) * HEAD_DIM ** -0.5
        p = jnp.exp(logits - jnp.max(logits, axis=-1, keepdims=True))
        den = jnp.sum(p, axis=-1, keepdims=True)
        att = jnp.dot(p.astype(BF16), mkv[:, MEM_WIDTH + lo:MEM_WIDTH + lo + HEAD_DIM],
                      preferred_element_type=F32)
        parts.append((att / den).astype(BF16))
    mix = jnp.concatenate(parts, axis=-1)
    o_ref[0] = x_ref[0] + jnp.dot(mix, wo_ref[...], preferred_element_type=F32)


def mix_out(y, q_mem, qm_block, mem_kv, w_out, layer, x, *, tm):
    b, t, d = x.shape
    return pl.pallas_call(
        _mix_out_kernel,
        grid=(b, t // tm),
        in_specs=[pl.BlockSpec((1, tm, y.shape[-1]), lambda bi, i: (bi, i, 0)),
                  pl.BlockSpec((1, tm, MEM_WIDTH), lambda bi, i: (bi, i, qm_block)),
                  pl.BlockSpec((None, 1) + mem_kv.shape[2:], lambda bi, i: (layer, bi, 0, 0)),
                  pl.BlockSpec((None,) + w_out.shape[1:], lambda bi, i: (layer, 0, 0)),
                  pl.BlockSpec((1, tm, d), lambda bi, i: (bi, i, 0))],
        out_specs=pl.BlockSpec((1, tm, d), lambda bi, i: (bi, i, 0)),
        out_shape=jax.ShapeDtypeStruct((b, t, d), F32),
        compiler_params=_cparams("parallel", "parallel"),
        name="mix_out",
    )(y, q_mem, mem_kv, w_out, x)


def _pack_attn_in(w_in):
    c_q, c_kv, k_idx, w_idx, q_mem = jnp.split(w_in, [512, 768, 832, 848], axis=-1)
    pad = jnp.zeros(w_in.shape[:-1] + (ATTN_IN_PAD - w_in.shape[-1],), w_in.dtype)
    return jnp.concatenate([c_q, c_kv, q_mem, k_idx, w_idx, pad], axis=-1)


def _pad_lanes(v, width):
    return jnp.pad(v, (0, width - v.shape[0])).reshape(1, width)


def kernel(x, mem, mixer_norm, ffn_norm, final_norm, mem_norm, w_mem_kv, w_out, w_gate, w_up, w_down,
           w_in_attn, q_norm, kv_norm, w_uq, w_uk, w_uv, w_idx_uq, idx_k_norm, idx_k_bias,
           w_in_pool, w_pool_group, pool_scale):
    b, t, d = x.shape
    depth = w_out.shape[0]
    n_mem = mem.shape[1]
    row = lambda v: v.reshape(1, -1)

    w_kv_b, w_out_b = w_mem_kv.astype(BF16), w_out.astype(BF16)
    w_gate_b, w_up_b, w_down_b = w_gate.astype(BF16), w_up.astype(BF16), w_down.astype(BF16)
    w_in_attn_b = _pack_attn_in(w_in_attn).astype(BF16)
    w_uq_t = jnp.swapaxes(w_uq, 1, 2).astype(BF16)
    w_iq_t = jnp.swapaxes(w_idx_uq, 1, 2).astype(BF16)
    w_uk_b, w_uv_b = w_uk.astype(BF16), w_uv.astype(BF16)
    w_in_pool_b, w_group_b = w_in_pool.astype(BF16), w_pool_group.astype(BF16)

    mem_kv = rms_matmul_layers(mem.reshape(b * n_mem, d), row(mem_norm), w_kv_b, tm=FFN_TM // 2, out_dtype=BF16)
    mem_kv = mem_kv.reshape(depth, b, n_mem, 2 * MEM_WIDTH)

    for i in range(depth):
        j = i // 2
        if i % 2 == 0:
            ckv, ckv_t, kidx, q_mem, q_abs_t, q_idx_t, w_idx_t = attn_proj(
                x, row(mixer_norm[i]), w_in_attn_b, row(q_norm[j]), row(kv_norm[j]),
                _pad_lanes(idx_k_norm[j], 128), _pad_lanes(idx_k_bias[j], 128), w_uq_t, w_uk_b, w_iq_t, j,
                tm=2 * DSA_TQ, tq=DSA_TQ)
            y = dsa_attention(q_idx_t, w_idx_t, q_abs_t, kidx, ckv, ckv_t, w_uv_b, j, tq=DSA_TQ, tk=DSA_TK)
            x = mix_out(y, q_mem, 0, mem_kv, w_out_b, i, x, tm=MIX_TM)
        else:
            proj = rms_matmul(x.reshape(b * t, d), row(mixer_norm[i]), w_in_pool_b, j,
                              tm=MIX_TM, out_dtype=F32).reshape(b, t, -1)
            y = pool_mix(proj, w_group_b, row(pool_scale[j]), j, tm=MIX_TM)
            x = mix_out(y, proj, POOL_WIDTH // MEM_WIDTH, mem_kv, w_out_b, i, x, tm=MIX_TM)
        x = ffn(x.reshape(b * t, d), row(ffn_norm[i]), w_gate_b, w_up_b, w_down_b, i, row(final_norm),
                tm=FFN_TM, tf=FFN_TF, final_norm=(i == depth - 1)).reshape(b, t, d)
    return x
```

```python
import functools

import jax
import jax.numpy as jnp
from jax import lax
from jax.experimental import pallas as pl
from jax.experimental.pallas import tpu as pltpu

BF16 = jnp.bfloat16
F32 = jnp.float32
I32 = jnp.int32

EPS = 1e-6
HEAD_DIM = 128
N_ATTN_HEADS = 12
Q_LORA = 512
KV_LORA = 256
IDX_HEADS = 16
IDX_DIM = 64
TOPK_MAX = 256
POOL_WINDOWS = (2, 4, 8, 16)
POOL_GROUP = 384
POOL_WIDTH = 1536
MEM_WIDTH = 512
N_MEM_HEADS = 4
POOL_HALO = 16

_CQ0, _CKV0, _QM0, _KI0, _WI0 = 0, 512, 768, 1280, 1344
ATTN_IN_PAD = 1408

LOG2E = 1.4426950408889634
INT_MIN = -(2 ** 31)
INT_MAX = 2 ** 31 - 1
BF16_MIN_NORMAL_BITS = 0x80
ONES_ROWS = 16
NEG = -1e30

VMEM_LIMIT = 56 * 1024 * 1024
MXU_COLS = 256
DSA_TQ = 256
DSA_TK = 512
FFN_TM, FFN_TF = 512, 512
MIX_TM = 512


def _cparams(*sem):
    return pltpu.CompilerParams(dimension_semantics=sem, vmem_limit_bytes=VMEM_LIMIT)


def _multi_chain_sum(parts, chains=4):
    accs = list(parts[:chains])
    for n, part in enumerate(parts[chains:]):
        accs[n % chains] = accs[n % chains] + part
    while len(accs) > 1:
        accs = [a + b for a, b in zip(accs[0::2], accs[1::2])] + accs[len(accs) & ~1:]
    return accs[0]


def _rms(xf, g):
    ms = jnp.mean(xf * xf, axis=-1, keepdims=True)
    return xf * lax.rsqrt(ms + EPS) * g


def _rms_matmul_kernel(x_ref, g_ref, w_ref, o_ref):
    h = _rms(x_ref[...], g_ref[...]).astype(BF16)
    o_ref[...] = jnp.dot(h, w_ref[...], preferred_element_type=F32).astype(o_ref.dtype)


def rms_matmul(x, g, w, layer, *, tm, out_dtype):
    m, d = x.shape
    n = w.shape[2]
    return pl.pallas_call(
        _rms_matmul_kernel,
        grid=(m // tm,),
        in_specs=[pl.BlockSpec((tm, d), lambda i: (i, 0)),
                  pl.BlockSpec((1, d), lambda i: (0, 0)),
                  pl.BlockSpec((None, d, n), lambda i: (layer, 0, 0))],
        out_specs=pl.BlockSpec((tm, n), lambda i: (i, 0)),
        out_shape=jax.ShapeDtypeStruct((m, n), out_dtype),
        compiler_params=_cparams("parallel"),
        name="rms_matmul",
    )(x, g, w)


def rms_matmul_layers(x, g, w, *, tm, out_dtype):
    m, d = x.shape
    n_layers, _, n = w.shape
    return pl.pallas_call(
        _rms_matmul_kernel,
        grid=(m // tm, n_layers),
        in_specs=[pl.BlockSpec((tm, d), lambda i, l: (i, 0)),
                  pl.BlockSpec((1, d), lambda i, l: (0, 0)),
                  pl.BlockSpec((None, d, n), lambda i, l: (l, 0, 0))],
        out_specs=pl.BlockSpec((None, tm, n), lambda i, l: (l, i, 0)),
        out_shape=jax.ShapeDtypeStruct((n_layers, m, n), out_dtype),
        compiler_params=_cparams("parallel", "parallel"),
        name="rms_matmul_layers",
    )(x, g, w)


def _ffn_kernel(x_ref, g_ref, wg_ref, wu_ref, wd_ref, fg_ref, o_ref, h_scr, *, final_norm):
    f = pl.program_id(1)

    @pl.when(f == 0)
    def _():
        x = x_ref[...]
        h_scr[...] = _rms(x, g_ref[...]).astype(BF16)
        o_ref[...] = x

    h = h_scr[...]
    gate = jnp.dot(h, wg_ref[...], preferred_element_type=F32)
    up = jnp.dot(h, wu_ref[...], preferred_element_type=F32)
    act = (gate * jax.nn.sigmoid(gate) * up).astype(BF16)
    o_ref[...] += jnp.dot(act, wd_ref[...], preferred_element_type=F32)

    if final_norm:
        @pl.when(f == pl.num_programs(1) - 1)
        def _():
            o_ref[...] = _rms(o_ref[...], fg_ref[...])


def ffn(x, g, w_gate, w_up, w_down, layer, final_g, *, tm, tf, final_norm):
    m, d = x.shape
    dff = w_gate.shape[2]
    return pl.pallas_call(
        functools.partial(_ffn_kernel, final_norm=final_norm),
        grid=(m // tm, dff // tf),
        in_specs=[pl.BlockSpec((tm, d), lambda i, f: (i, 0)),
                  pl.BlockSpec((1, d), lambda i, f: (0, 0)),
                  pl.BlockSpec((None, d, tf), lambda i, f: (layer, 0, f)),
                  pl.BlockSpec((None, d, tf), lambda i, f: (layer, 0, f)),
                  pl.BlockSpec((None, tf, d), lambda i, f: (layer, f, 0)),
                  pl.BlockSpec((1, d), lambda i, f: (0, 0))],
        out_specs=pl.BlockSpec((tm, d), lambda i, f: (i, 0)),
        out_shape=jax.ShapeDtypeStruct((m, d), F32),
        scratch_shapes=[pltpu.VMEM((tm, d), BF16)],
        compiler_params=_cparams("parallel", "arbitrary"),
        name="ffn",
    )(x, g, w_gate, w_up, w_down, final_g)


def _attn_proj_kernel(x_ref, g_ref, win_ref, qn_ref, kvn_ref, ig_ref, ib_ref, wuqt_ref, wuk_ref, wiqt_ref,
                      ckv_ref, ckvt_ref, kidx_ref, qmem_ref, qabst_ref, qidxt_ref, widxt_ref, *, tq):
    tm = x_ref.shape[1]
    h = _rms(x_ref[0], g_ref[...]).astype(BF16)
    proj = jnp.dot(h, win_ref[...], preferred_element_type=F32)
    c_q_t = _rms(proj[:, _CQ0:_CQ0 + Q_LORA], qn_ref[...]).T.astype(BF16)
    c_kv = _rms(proj[:, _CKV0:_CKV0 + KV_LORA], kvn_ref[...])
    ckv_ref[0] = c_kv.astype(BF16)
    ckvt_ref[0, 0:KV_LORA, :] = c_kv.T.astype(BF16)
    ckvt_ref[0, KV_LORA:KV_LORA + ONES_ROWS, :] = jnp.ones((ONES_ROWS, tm), BF16)
    qmem_ref[0] = proj[:, _QM0:_QM0 + MEM_WIDTH].astype(BF16)
    slab = proj[:, _KI0:_KI0 + 128]
    live = lax.broadcasted_iota(I32, slab.shape, 1) < IDX_DIM
    mu = jnp.sum(jnp.where(live, slab, 0.0), axis=-1, keepdims=True) * (1.0 / IDX_DIM)
    cen = jnp.where(live, slab - mu, 0.0)
    var = jnp.sum(cen * cen, axis=-1, keepdims=True) * (1.0 / IDX_DIM)
    kin = cen * lax.rsqrt(var + EPS) * ig_ref[...] + ib_ref[...]
    kidx_ref[0] = kin[:, :IDX_DIM].astype(BF16)
    widxt_ref[0] = slab.T[IDX_DIM:IDX_DIM + IDX_HEADS, :] * (IDX_HEADS ** -0.5 * IDX_DIM ** -0.5)
    q_t = jnp.dot(wuqt_ref[...], c_q_t, preferred_element_type=F32).astype(BF16)
    for hd in range(N_ATTN_HEADS):
        qa_t = jnp.dot(wuk_ref[hd], q_t[hd * HEAD_DIM:(hd + 1) * HEAD_DIM, :], preferred_element_type=F32)
        qa_t = (qa_t * (HEAD_DIM ** -0.5 * LOG2E)).astype(BF16)
        for j in range(tm // tq):
            qabst_ref[0, j, :, hd * tq:(hd + 1) * tq] = qa_t[:, j * tq:(j + 1) * tq]
    qi_t = jnp.dot(wiqt_ref[...], c_q_t, preferred_element_type=F32).astype(BF16)
    for hd in range(IDX_HEADS):
        for j in range(tm // tq):
            qidxt_ref[0, j, :, hd * tq:(hd + 1) * tq] = qi_t[hd * IDX_DIM:(hd + 1) * IDX_DIM, j * tq:(j + 1) * tq]


def attn_proj(x, g, w_in, q_norm, kv_norm, idx_g, idx_b, w_uq_t, w_uk, w_iq_t, layer, *, tm, tq):
    b, t, d = x.shape
    full = lambda shape: pl.BlockSpec(shape, lambda bi, i: (0,) * len(shape))
    stacked = lambda w: pl.BlockSpec((None,) + w.shape[1:], lambda bi, i: (layer,) + (0,) * (w.ndim - 1))
    per_tile = tm // tq
    return pl.pallas_call(
        functools.partial(_attn_proj_kernel, tq=tq),
        grid=(b, t // tm),
        in_specs=[pl.BlockSpec((1, tm, d), lambda bi, i: (bi, i, 0)),
                  full((1, d)), stacked(w_in), full((1, Q_LORA)), full((1, KV_LORA)),
                  full((1, 128)), full((1, 128)), stacked(w_uq_t), stacked(w_uk), stacked(w_iq_t)],
        out_specs=[pl.BlockSpec((1, tm, KV_LORA), lambda bi, i: (bi, i, 0)),
                   pl.BlockSpec((1, KV_LORA + ONES_ROWS, tm), lambda bi, i: (bi, 0, i)),
                   pl.BlockSpec((1, tm, IDX_DIM), lambda bi, i: (bi, i, 0)),
                   pl.BlockSpec((1, tm, MEM_WIDTH), lambda bi, i: (bi, i, 0)),
                   pl.BlockSpec((1, per_tile, KV_LORA, N_ATTN_HEADS * tq), lambda bi, i: (bi, i, 0, 0)),
                   pl.BlockSpec((1, per_tile, IDX_DIM, IDX_HEADS * tq), lambda bi, i: (bi, i, 0, 0)),
                   pl.BlockSpec((1, IDX_HEADS, tm), lambda bi, i: (bi, 0, i))],
        out_shape=[jax.ShapeDtypeStruct((b, t, KV_LORA), BF16),
                   jax.ShapeDtypeStruct((b, KV_LORA + ONES_ROWS, t), BF16),
                   jax.ShapeDtypeStruct((b, t, IDX_DIM), BF16),
                   jax.ShapeDtypeStruct((b, t, MEM_WIDTH), BF16),
                   jax.ShapeDtypeStruct((b, t // tq, KV_LORA, N_ATTN_HEADS * tq), BF16),
                   jax.ShapeDtypeStruct((b, t // tq, IDX_DIM, IDX_HEADS * tq), BF16),
                   jax.ShapeDtypeStruct((b, IDX_HEADS, t), F32)],
        compiler_params=_cparams("parallel", "parallel"),
        name="attn_proj",
    )(x, g, w_in, q_norm, kv_norm, idx_g, idx_b, w_uq_t, w_uk, w_iq_t)


def _dsa_kernel(qidxt_ref, widxt_ref, qabst_ref, kidx_ref, ckv_ref, ckvt_ref, wuv_ref, y_ref,
                key_scr, top_scr, m_scr, acc_scr, *, tq, tk, k_top):
    i = pl.program_id(1)
    q0 = i * tq
    n_chunks = (q0 + tq + tk - 1) // tk
    qpos = q0 + lax.broadcasted_iota(I32, (1, tq), 1)
    hpt = max(1, MXU_COLS // tq)

    w = widxt_ref[0]

    def score_body(c, carry):
        k0 = pl.multiple_of(c * tk, tk)
        kc = kidx_ref[0, pl.ds(k0, tk), :]
        score = None
        for j in range(IDX_HEADS // hpt):
            s = jnp.dot(kc, qidxt_ref[0, 0, :, j * hpt * tq:(j + 1) * hpt * tq], preferred_element_type=F32)
            for u in range(hpt):
                hd = hpt * j + u
                term = jnp.maximum(s[:, u * tq:(u + 1) * tq], 0.0) * w[hd:hd + 1, :]
                score = term if score is None else score + term
        score = jnp.where(score == 0.0, 0.0, score)
        bits = pltpu.bitcast(score, I32)
        key = bits ^ ((bits >> 31) & 0x7FFFFFFF)
        causal = k0 + lax.broadcasted_iota(I32, (tk, 1), 0) <= qpos
        key_scr[pl.ds(k0, tk), :] = jnp.where(causal, key, INT_MIN)
        top = pltpu.bitcast(bits & jnp.int32(-65536), F32)
        top_scr[pl.ds(k0, tk), :] = jnp.where(causal, top, -jnp.inf).astype(BF16)
        return carry

    lax.fori_loop(0, n_chunks, score_body, 0)

    def count_rows(hit_fn):
        def body(c, acc):
            k0 = pl.multiple_of(c * tk, tk)
            kpos = k0 + lax.broadcasted_iota(I32, (tk, 1), 0)
            hit = jnp.where(hit_fn(key_scr[pl.ds(k0, tk), :], kpos), 1, 0).reshape(tk // 8, 8, tq)
            return acc + _multi_chain_sum([hit[g] for g in range(tk // 8)])
        acc = lax.fori_loop(0, n_chunks, body, jnp.zeros((8, tq), I32))
        return jnp.sum(acc, axis=0, keepdims=True)

    def count_ge(cand):
        return count_rows(lambda kk, kpos: kk >= cand)

    def count_top_ge(cand):
        c16 = cand >> 16
        c16 = jnp.where((c16 > 0) & (c16 < BF16_MIN_NORMAL_BITS), BF16_MIN_NORMAL_BITS, c16)
        cand_top = pltpu.bitcast((c16 ^ ((c16 >> 15) & 0x7FFF)) << 16, F32).astype(BF16)
        one, zero = jnp.ones((), BF16), jnp.zeros((), BF16)

        def body(c, acc):
            k0 = pl.multiple_of(c * tk, tk)
            hit = jnp.where(top_scr[pl.ds(k0, tk), :] >= cand_top, one, zero).reshape(tk // 16, 16, tq)
            part = _multi_chain_sum([hit[g] for g in range(tk // 16)])
            return acc + part.astype(F32)
        acc = lax.fori_loop(0, n_chunks, body, jnp.zeros((16, tq), F32))
        return jnp.sum(acc, axis=0, keepdims=True).astype(I32)

    def search_body(count_fn, b, state):
        thr, cnt = state
        cand = thr + lax.shift_left(jnp.int32(1), 31 - b)
        c = count_fn(cand)
        take = c >= k_top
        return jnp.where(take, cand, thr), jnp.where(take, c, cnt)

    state = (jnp.full((1, tq), INT_MIN, I32), jnp.full((1, tq), -1, I32))
    state = lax.fori_loop(0, 16, functools.partial(search_body, count_top_ge), state)
    thr, cnt = lax.fori_loop(16, 32, functools.partial(search_body, count_ge), state)

    @pl.when(jnp.sum(jnp.where(cnt > k_top, 1, 0)) > 0)
    def _():
        at_max = thr == INT_MAX
        above = jnp.where(at_max, 0, count_ge(jnp.where(at_max, thr, thr + 1)))
        need = k_top - above
        pos_bits = max(1, (key_scr.shape[0] - 1).bit_length())

        def pos_body(b, lo):
            cand = lo + lax.shift_left(jnp.int32(1), pos_bits - 1 - b)
            before = count_rows(lambda kk, kpos: (kk == thr) & (kpos < cand))
            return jnp.where(before < need, cand, lo)

        last = lax.fori_loop(0, pos_bits, pos_body, jnp.zeros((1, tq), I32))

        def drop_body(c, carry):
            k0 = pl.multiple_of(c * tk, tk)
            kpos = k0 + lax.broadcasted_iota(I32, (tk, 1), 0)
            kk = key_scr[pl.ds(k0, tk), :]
            key_scr[pl.ds(k0, tk), :] = jnp.where((kk == thr) & (kpos > last), INT_MIN, kk)
            return carry

        lax.fori_loop(0, n_chunks, drop_body, 0)

    thr = jnp.maximum(thr, INT_MIN + 1)

    m_scr[...] = jnp.full(m_scr.shape, NEG, F32)
    acc_scr[...] = jnp.zeros(acc_scr.shape, F32)

    def att_body(c, carry):
        k0 = pl.multiple_of(c * tk, tk)
        kv = ckv_ref[0, pl.ds(k0, tk), :]
        kv_t = ckvt_ref[0, :, pl.ds(k0, tk)]
        bias = jnp.where(key_scr[pl.ds(k0, tk), :] >= thr, 0.0, NEG)
        for j in range(N_ATTN_HEADS // hpt):
            cols = slice(j * hpt * tq, (j + 1) * hpt * tq)
            lt = jnp.dot(kv, qabst_ref[0, 0, :, cols], preferred_element_type=F32)
            ps, alphas = [], []
            for u in range(hpt):
                hc = slice((hpt * j + u) * tq, (hpt * j + u + 1) * tq)
                lg = lt[:, u * tq:(u + 1) * tq] + bias
                m_prev = m_scr[:, hc]
                m_new = jnp.maximum(m_prev, jnp.max(lg, axis=0, keepdims=True))
                ps.append(jnp.exp2(lg - m_new).astype(BF16))
                alphas.append(jnp.exp2(m_prev - m_new))
                m_scr[:, hc] = m_new
            pv = jnp.dot(kv_t, jnp.concatenate(ps, axis=1), preferred_element_type=F32)
            acc_scr[:, cols] = jnp.concatenate(alphas, axis=1) * acc_scr[:, cols] + pv
        return carry

    lax.fori_loop(0, n_chunks, att_body, 0)

    for hd in range(N_ATTN_HEADS):
        hc = slice(hd * tq, (hd + 1) * tq)
        den = acc_scr[KV_LORA:KV_LORA + 1, hc]
        o_lat = (acc_scr[0:KV_LORA, hc] * (1.0 / den)).T.astype(BF16)
        y_ref[0, :, hd * HEAD_DIM:(hd + 1) * HEAD_DIM] = jnp.dot(
            o_lat, wuv_ref[hd], preferred_element_type=F32).astype(y_ref.dtype)


def dsa_attention(q_idx_t, w_idx_t, q_abs_t, k_idx, c_kv, c_kv_t, w_uv, layer, *, tq, tk):
    b, t, _ = c_kv.shape
    k_top = min(TOPK_MAX, t // 4)
    return pl.pallas_call(
        functools.partial(_dsa_kernel, tq=tq, tk=tk, k_top=k_top),
        grid=(b, t // tq),
        in_specs=[pl.BlockSpec((1, 1, IDX_DIM, IDX_HEADS * tq), lambda bi, i: (bi, i, 0, 0)),
                  pl.BlockSpec((1, IDX_HEADS, tq), lambda bi, i: (bi, 0, i)),
                  pl.BlockSpec((1, 1, KV_LORA, N_ATTN_HEADS * tq), lambda bi, i: (bi, i, 0, 0)),
                  pl.BlockSpec((1, t, IDX_DIM), lambda bi, i: (bi, 0, 0)),
                  pl.BlockSpec((1, t, KV_LORA), lambda bi, i: (bi, 0, 0)),
                  pl.BlockSpec((1, KV_LORA + ONES_ROWS, t), lambda bi, i: (bi, 0, 0)),
                  pl.BlockSpec((None,) + w_uv.shape[1:], lambda bi, i: (layer, 0, 0, 0))],
        out_specs=pl.BlockSpec((1, tq, N_ATTN_HEADS * HEAD_DIM), lambda bi, i: (bi, i, 0)),
        out_shape=jax.ShapeDtypeStruct((b, t, N_ATTN_HEADS * HEAD_DIM), BF16),
        scratch_shapes=[pltpu.VMEM((t, tq), I32),
                        pltpu.VMEM((t, tq), BF16),
                        pltpu.VMEM((1, N_ATTN_HEADS * tq), F32),
                        pltpu.VMEM((KV_LORA + ONES_ROWS, N_ATTN_HEADS * tq), F32)],
        compiler_params=_cparams("parallel", "parallel"),
        name="dsa_attention",
    )(q_idx_t, w_idx_t, q_abs_t, k_idx, c_kv, c_kv_t, w_uv)


def _pool_kernel(u_ref, halo_ref, wg_ref, sc_ref, y_ref, ext_scr, *, tm):
    i = pl.program_id(1)
    halo = halo_ref[0]
    ext_scr[0:POOL_HALO, :] = jnp.where(i > 0, halo, jnp.zeros_like(halo))
    ext_scr[POOL_HALO:POOL_HALO + tm, :] = u_ref[0]
    pos = i * tm + lax.broadcasted_iota(I32, (tm, 1), 0)
    for g, win in enumerate(POOL_WINDOWS):
        lo = g * POOL_GROUP
        tok = ext_scr[POOL_HALO:POOL_HALO + tm, lo:lo + POOL_GROUP]
        total = tok
        for k in range(1, win):
            total = total + ext_scr[POOL_HALO - k:POOL_HALO - k + tm, lo:lo + POOL_GROUP]
        cnt = jnp.minimum(pos + 1, win).astype(F32)
        mixed = (total / cnt - tok).astype(BF16)
        out = jnp.dot(mixed, wg_ref[g], preferred_element_type=F32)
        y_ref[0, :, lo:lo + POOL_GROUP] = (out * sc_ref[:, lo:lo + POOL_GROUP]).astype(y_ref.dtype)


def pool_mix(proj, w_group, scale, layer, *, tm):
    b, t, _ = proj.shape
    halo_blocks = tm // POOL_HALO
    return pl.pallas_call(
        functools.partial(_pool_kernel, tm=tm),
        grid=(b, t // tm),
        in_specs=[pl.BlockSpec((1, tm, POOL_WIDTH), lambda bi, i: (bi, i, 0)),
                  pl.BlockSpec((1, POOL_HALO, POOL_WIDTH),
                               lambda bi, i: (bi, jnp.maximum(i * halo_blocks - 1, 0), 0)),
                  pl.BlockSpec((None,) + w_group.shape[1:], lambda bi, i: (layer, 0, 0, 0)),
                  pl.BlockSpec((1, POOL_WIDTH), lambda bi, i: (0, 0))],
        out_specs=pl.BlockSpec((1, tm, POOL_WIDTH), lambda bi, i: (bi, i, 0)),
        out_shape=jax.ShapeDtypeStruct((b, t, POOL_WIDTH), BF16),
        scratch_shapes=[pltpu.VMEM((POOL_HALO + tm, POOL_WIDTH), F32)],
        compiler_params=_cparams("parallel", "parallel"),
        name="pool_mix",
    )(proj, proj, w_group, scale)


def _mix_out_kernel(y_ref, qm_ref, mkv_ref, wo_ref, x_ref, o_ref):
    qm = qm_ref[0].astype(BF16)
    mkv = mkv_ref[0]
    parts = [y_ref[0]]
    for hd in range(N_MEM_HEADS):
        lo = hd * HEAD_DIM
        logits = lax.dot_general(qm[:, lo:lo + HEAD_DIM], mkv[:, lo:lo + HEAD_DIM],
                                 (((1,), (1,)), ((), ())), preferred_element_type=F32) * HEAD_DIM ** -0.5
        p = jnp.exp(logits - jnp.max(logits, axis=-1, keepdims=True))
        den = jnp.sum(p, axis=-1, keepdims=True)
        att = jnp.dot(p.astype(BF16), mkv[:, MEM_WIDTH + lo:MEM_WIDTH + lo + HEAD_DIM],
                      preferred_element_type=F32)
        parts.append((att / den).astype(BF16))
    mix = jnp.concatenate(parts, axis=-1)
    o_ref[0] = x_ref[0] + jnp.dot(mix, wo_ref[...], preferred_element_type=F32)


def mix_out(y, q_mem, qm_block, mem_kv, w_out, layer, x, *, tm):
    b, t, d = x.shape
    return pl.pallas_call(
        _mix_out_kernel,
        grid=(b, t // tm),
        in_specs=[pl.BlockSpec((1, tm, y.shape[-1]), lambda bi, i: (bi, i, 0)),
                  pl.BlockSpec((1, tm, MEM_WIDTH), lambda bi, i: (bi, i, qm_block)),
                  pl.BlockSpec((None, 1) + mem_kv.shape[2:], lambda bi, i: (layer, bi, 0, 0)),
                  pl.BlockSpec((None,) + w_out.shape[1:], lambda bi, i: (layer, 0, 0)),
                  pl.BlockSpec((1, tm, d), lambda bi, i: (bi, i, 0))],
        out_specs=pl.BlockSpec((1, tm, d), lambda bi, i: (bi, i, 0)),
        out_shape=jax.ShapeDtypeStruct((b, t, d), F32),
        compiler_params=_cparams("parallel", "parallel"),
        name="mix_out",
    )(y, q_mem, mem_kv, w_out, x)


def _pack_attn_in(w_in):
    c_q, c_kv, k_idx, w_idx, q_mem = jnp.split(w_in, [512, 768, 832, 848], axis=-1)
    pad = jnp.zeros(w_in.shape[:-1] + (ATTN_IN_PAD - w_in.shape[-1],), w_in.dtype)
    return jnp.concatenate([c_q, c_kv, q_mem, k_idx, w_idx, pad], axis=-1)


def _pad_lanes(v, width):
    return jnp.pad(v, (0, width - v.shape[0])).reshape(1, width)


def kernel(x, mem, mixer_norm, ffn_norm, final_norm, mem_norm, w_mem_kv, w_out, w_gate, w_up, w_down,
           w_in_attn, q_norm, kv_norm, w_uq, w_uk, w_uv, w_idx_uq, idx_k_norm, idx_k_bias,
           w_in_pool, w_pool_group, pool_scale):
    b, t, d = x.shape
    depth = w_out.shape[0]
    n_mem = mem.shape[1]
    row = lambda v: v.reshape(1, -1)

    w_kv_b, w_out_b = w_mem_kv.astype(BF16), w_out.astype(BF16)
    w_gate_b, w_up_b, w_down_b = w_gate.astype(BF16), w_up.astype(BF16), w_down.astype(BF16)
    w_in_attn_b = _pack_attn_in(w_in_attn).astype(BF16)
    w_uq_t = jnp.swapaxes(w_uq, 1, 2).astype(BF16)
    w_iq_t = jnp.swapaxes(w_idx_uq, 1, 2).astype(BF16)
    w_uk_b, w_uv_b = w_uk.astype(BF16), w_uv.astype(BF16)
    w_in_pool_b, w_group_b = w_in_pool.astype(BF16), w_pool_group.astype(BF16)

    mem_kv = rms_matmul_layers(mem.reshape(b * n_mem, d), row(mem_norm), w_kv_b, tm=FFN_TM // 2, out_dtype=BF16)
    mem_kv = mem_kv.reshape(depth, b, n_mem, 2 * MEM_WIDTH)

    for i in range(depth):
        j = i // 2
        if i % 2 == 0:
            ckv, ckv_t, kidx, q_mem, q_abs_t, q_idx_t, w_idx_t = attn_proj(
                x, row(mixer_norm[i]), w_in_attn_b, row(q_norm[j]), row(kv_norm[j]),
                _pad_lanes(idx_k_norm[j], 128), _pad_lanes(idx_k_bias[j], 128), w_uq_t, w_uk_b, w_iq_t, j,
                tm=max(256, DSA_TQ), tq=DSA_TQ)
            y = dsa_attention(q_idx_t, w_idx_t, q_abs_t, kidx, ckv, ckv_t, w_uv_b, j, tq=DSA_TQ, tk=DSA_TK)
            x = mix_out(y, q_mem, 0, mem_kv, w_out_b, i, x, tm=MIX_TM)
        else:
            proj = rms_matmul(x.reshape(b * t, d), row(mixer_norm[i]), w_in_pool_b, j,
                              tm=MIX_TM, out_dtype=F32).reshape(b, t, -1)
            y = pool_mix(proj, w_group_b, row(pool_scale[j]), j, tm=MIX_TM)
            x = mix_out(y, proj, POOL_WIDTH // MEM_WIDTH, mem_kv, w_out_b, i, x, tm=MIX_TM)
        x = ffn(x.reshape(b * t, d), row(ffn_norm[i]), w_gate_b, w_up_b, w_down_b, i, row(final_norm),
                tm=FFN_TM, tf=FFN_TF, final_norm=(i == depth - 1)).reshape(b, t, d)
    return x
```

```python
import functools

import jax
import jax.numpy as jnp
from jax import lax
from jax.experimental import pallas as pl
from jax.experimental.pallas import tpu as pltpu

BF16 = jnp.bfloat16
F32 = jnp.float32
I32 = jnp.int32
I16 = jnp.int16

EPS = 1e-6
HEAD_DIM = 128
N_ATTN_HEADS = 12
Q_LORA = 512
KV_LORA = 256
IDX_HEADS = 16
IDX_DIM = 64
TOPK_MAX = 256
POOL_WINDOWS = (2, 4, 8, 16)
POOL_GROUP = 384
POOL_WIDTH = 1536
MEM_WIDTH = 512
N_MEM_HEADS = 4
POOL_HALO = 16

_CQ0, _CKV0, _QM0, _KI0, _WI0 = 0, 512, 768, 1280, 1344
ATTN_IN_PAD = 1408

LOG2E = 1.4426950408889634
INT_MIN = -(2 ** 31)
INT_MAX = 2 ** 31 - 1
I16_MIN = -(2 ** 15)
ONES_ROWS = 16
NEG = -1e30

VMEM_LIMIT = 56 * 1024 * 1024
MXU_COLS = 256
DSA_TQ = 256
DSA_TK = 512
FFN_TM, FFN_TF = 1024, 256
MIX_TM = 512


def _cparams(*sem):
    return pltpu.CompilerParams(dimension_semantics=sem, vmem_limit_bytes=VMEM_LIMIT)


def _multi_chain_sum(parts, chains=4):
    accs = list(parts[:chains])
    for n, part in enumerate(parts[chains:]):
        accs[n % chains] = accs[n % chains] + part
    while len(accs) > 1:
        accs = [a + b for a, b in zip(accs[0::2], accs[1::2])] + accs[len(accs) & ~1:]
    return accs[0]


def _rms(xf, g):
    ms = jnp.mean(xf * xf, axis=-1, keepdims=True)
    return xf * lax.rsqrt(ms + EPS) * g


def _rms_matmul_kernel(x_ref, g_ref, w_ref, o_ref):
    h = _rms(x_ref[...], g_ref[...]).astype(BF16)
    o_ref[...] = jnp.dot(h, w_ref[...], preferred_element_type=F32).astype(o_ref.dtype)


def rms_matmul(x, g, w, layer, *, tm, out_dtype):
    m, d = x.shape
    n = w.shape[2]
    return pl.pallas_call(
        _rms_matmul_kernel,
        grid=(m // tm,),
        in_specs=[pl.BlockSpec((tm, d), lambda i: (i, 0)),
                  pl.BlockSpec((1, d), lambda i: (0, 0)),
                  pl.BlockSpec((None, d, n), lambda i: (layer, 0, 0))],
        out_specs=pl.BlockSpec((tm, n), lambda i: (i, 0)),
        out_shape=jax.ShapeDtypeStruct((m, n), out_dtype),
        compiler_params=_cparams("parallel"),
        name="rms_matmul",
    )(x, g, w)


def rms_matmul_layers(x, g, w, *, tm, out_dtype):
    m, d = x.shape
    n_layers, _, n = w.shape
    return pl.pallas_call(
        _rms_matmul_kernel,
        grid=(m // tm, n_layers),
        in_specs=[pl.BlockSpec((tm, d), lambda i, l: (i, 0)),
                  pl.BlockSpec((1, d), lambda i, l: (0, 0)),
                  pl.BlockSpec((None, d, n), lambda i, l: (l, 0, 0))],
        out_specs=pl.BlockSpec((None, tm, n), lambda i, l: (l, i, 0)),
        out_shape=jax.ShapeDtypeStruct((n_layers, m, n), out_dtype),
        compiler_params=_cparams("parallel", "parallel"),
        name="rms_matmul_layers",
    )(x, g, w)


def _ffn_kernel(x_ref, g_ref, wg_ref, wu_ref, wd_ref, fg_ref, o_ref, h_scr, *, final_norm):
    f = pl.program_id(1)

    @pl.when(f == 0)
    def _():
        x = x_ref[...]
        h_scr[...] = _rms(x, g_ref[...]).astype(BF16)
        o_ref[...] = x

    h = h_scr[...]
    gate = jnp.dot(h, wg_ref[...], preferred_element_type=F32)
    up = jnp.dot(h, wu_ref[...], preferred_element_type=F32)
    act = (gate * jax.nn.sigmoid(gate) * up).astype(BF16)
    o_ref[...] += jnp.dot(act, wd_ref[...], preferred_element_type=F32)

    if final_norm:
        @pl.when(f == pl.num_programs(1) - 1)
        def _():
            o_ref[...] = _rms(o_ref[...], fg_ref[...])


def ffn(x, g, w_gate, w_up, w_down, layer, final_g, *, tm, tf, final_norm):
    m, d = x.shape
    dff = w_gate.shape[2]
    return pl.pallas_call(
        functools.partial(_ffn_kernel, final_norm=final_norm),
        grid=(m // tm, dff // tf),
        in_specs=[pl.BlockSpec((tm, d), lambda i, f: (i, 0)),
                  pl.BlockSpec((1, d), lambda i, f: (0, 0)),
                  pl.BlockSpec((None, d, tf), lambda i, f: (layer, 0, f)),
                  pl.BlockSpec((None, d, tf), lambda i, f: (layer, 0, f)),
                  pl.BlockSpec((None, tf, d), lambda i, f: (layer, f, 0)),
                  pl.BlockSpec((1, d), lambda i, f: (0, 0))],
        out_specs=pl.BlockSpec((tm, d), lambda i, f: (i, 0)),
        out_shape=jax.ShapeDtypeStruct((m, d), F32),
        scratch_shapes=[pltpu.VMEM((tm, d), BF16)],
        compiler_params=_cparams("parallel", "arbitrary"),
        name="ffn",
    )(x, g, w_gate, w_up, w_down, final_g)


def _attn_proj_kernel(x_ref, g_ref, win_ref, qn_ref, kvn_ref, ig_ref, ib_ref, wuqt_ref, wuk_ref, wiqt_ref,
                      ckv_ref, ckvt_ref, kidx_ref, qmem_ref, qabst_ref, qidxt_ref, widxt_ref, *, tq):
    tm = x_ref.shape[1]
    h = _rms(x_ref[0], g_ref[...]).astype(BF16)
    proj = jnp.dot(h, win_ref[...], preferred_element_type=F32)
    c_q_t = _rms(proj[:, _CQ0:_CQ0 + Q_LORA], qn_ref[...]).T.astype(BF16)
    c_kv = _rms(proj[:, _CKV0:_CKV0 + KV_LORA], kvn_ref[...])
    ckv_ref[0] = c_kv.astype(BF16)
    ckvt_ref[0, 0:KV_LORA, :] = c_kv.T.astype(BF16)
    ckvt_ref[0, KV_LORA:KV_LORA + ONES_ROWS, :] = jnp.ones((ONES_ROWS, tm), BF16)
    qmem_ref[0] = proj[:, _QM0:_QM0 + MEM_WIDTH].astype(BF16)
    slab = proj[:, _KI0:_KI0 + 128]
    live = lax.broadcasted_iota(I32, slab.shape, 1) < IDX_DIM
    mu = jnp.sum(jnp.where(live, slab, 0.0), axis=-1, keepdims=True) * (1.0 / IDX_DIM)
    cen = jnp.where(live, slab - mu, 0.0)
    var = jnp.sum(cen * cen, axis=-1, keepdims=True) * (1.0 / IDX_DIM)
    kin = cen * lax.rsqrt(var + EPS) * ig_ref[...] + ib_ref[...]
    kidx_ref[0] = kin[:, :IDX_DIM].astype(BF16)
    widxt_ref[0] = slab.T[IDX_DIM:IDX_DIM + IDX_HEADS, :] * (IDX_HEADS ** -0.5 * IDX_DIM ** -0.5)
    q_t = jnp.dot(wuqt_ref[...], c_q_t, preferred_element_type=F32).astype(BF16)
    for hd in range(N_ATTN_HEADS):
        qa_t = jnp.dot(wuk_ref[hd], q_t[hd * HEAD_DIM:(hd + 1) * HEAD_DIM, :], preferred_element_type=F32)
        qa_t = (qa_t * (HEAD_DIM ** -0.5 * LOG2E)).astype(BF16)
        for j in range(tm // tq):
            qabst_ref[0, j, :, hd * tq:(hd + 1) * tq] = qa_t[:, j * tq:(j + 1) * tq]
    qi_t = jnp.dot(wiqt_ref[...], c_q_t, preferred_element_type=F32).astype(BF16)
    for hd in range(IDX_HEADS):
        for j in range(tm // tq):
            qidxt_ref[0, j, :, hd * tq:(hd + 1) * tq] = qi_t[hd * IDX_DIM:(hd + 1) * IDX_DIM, j * tq:(j + 1) * tq]


def attn_proj(x, g, w_in, q_norm, kv_norm, idx_g, idx_b, w_uq_t, w_uk, w_iq_t, layer, *, tm, tq):
    b, t, d = x.shape
    full = lambda shape: pl.BlockSpec(shape, lambda bi, i: (0,) * len(shape))
    stacked = lambda w: pl.BlockSpec((None,) + w.shape[1:], lambda bi, i: (layer,) + (0,) * (w.ndim - 1))
    per_tile = tm // tq
    return pl.pallas_call(
        functools.partial(_attn_proj_kernel, tq=tq),
        grid=(b, t // tm),
        in_specs=[pl.BlockSpec((1, tm, d), lambda bi, i: (bi, i, 0)),
                  full((1, d)), stacked(w_in), full((1, Q_LORA)), full((1, KV_LORA)),
                  full((1, 128)), full((1, 128)), stacked(w_uq_t), stacked(w_uk), stacked(w_iq_t)],
        out_specs=[pl.BlockSpec((1, tm, KV_LORA), lambda bi, i: (bi, i, 0)),
                   pl.BlockSpec((1, KV_LORA + ONES_ROWS, tm), lambda bi, i: (bi, 0, i)),
                   pl.BlockSpec((1, tm, IDX_DIM), lambda bi, i: (bi, i, 0)),
                   pl.BlockSpec((1, tm, MEM_WIDTH), lambda bi, i: (bi, i, 0)),
                   pl.BlockSpec((1, per_tile, KV_LORA, N_ATTN_HEADS * tq), lambda bi, i: (bi, i, 0, 0)),
                   pl.BlockSpec((1, per_tile, IDX_DIM, IDX_HEADS * tq), lambda bi, i: (bi, i, 0, 0)),
                   pl.BlockSpec((1, IDX_HEADS, tm), lambda bi, i: (bi, 0, i))],
        out_shape=[jax.ShapeDtypeStruct((b, t, KV_LORA), BF16),
                   jax.ShapeDtypeStruct((b, KV_LORA + ONES_ROWS, t), BF16),
                   jax.ShapeDtypeStruct((b, t, IDX_DIM), BF16),
                   jax.ShapeDtypeStruct((b, t, MEM_WIDTH), BF16),
                   jax.ShapeDtypeStruct((b, t // tq, KV_LORA, N_ATTN_HEADS * tq), BF16),
                   jax.ShapeDtypeStruct((b, t // tq, IDX_DIM, IDX_HEADS * tq), BF16),
                   jax.ShapeDtypeStruct((b, IDX_HEADS, t), F32)],
        compiler_params=_cparams("parallel", "parallel"),
        name="attn_proj",
    )(x, g, w_in, q_norm, kv_norm, idx_g, idx_b, w_uq_t, w_uk, w_iq_t)


def _dsa_kernel(qidxt_ref, widxt_ref, qabst_ref, kidx_ref, ckv_ref, ckvt_ref, wuv_ref, y_ref,
                key_scr, half_scr, m_scr, acc_scr, *, tq, tk, k_top):
    i = pl.program_id(1)
    q0 = i * tq
    n_chunks = (q0 + tq + tk - 1) // tk
    qpos = q0 + lax.broadcasted_iota(I32, (1, tq), 1)
    hpt = max(1, MXU_COLS // tq)

    w = widxt_ref[0]

    def score_body(c, carry):
        k0 = pl.multiple_of(c * tk, tk)
        kc = kidx_ref[0, pl.ds(k0, tk), :]
        score = None
        for j in range(IDX_HEADS // hpt):
            s = jnp.dot(kc, qidxt_ref[0, 0, :, j * hpt * tq:(j + 1) * hpt * tq], preferred_element_type=F32)
            for u in range(hpt):
                hd = hpt * j + u
                term = jnp.maximum(s[:, u * tq:(u + 1) * tq], 0.0) * w[hd:hd + 1, :]
                score = term if score is None else score + term
        score = jnp.where(score == 0.0, 0.0, score)
        bits = pltpu.bitcast(score, I32)
        key = bits ^ ((bits >> 31) & 0x7FFFFFFF)
        causal = k0 + lax.broadcasted_iota(I32, (tk, 1), 0) <= qpos
        key_scr[pl.ds(k0, tk), :] = jnp.where(causal, key, INT_MIN)
        half_scr[pl.ds(k0, tk), :] = (jnp.where(causal, key, INT_MIN) >> 16).astype(I16)
        return carry

    lax.fori_loop(0, n_chunks, score_body, 0)

    def count_rows(hit_fn):
        def body(c, acc):
            k0 = pl.multiple_of(c * tk, tk)
            kpos = k0 + lax.broadcasted_iota(I32, (tk, 1), 0)
            hit = jnp.where(hit_fn(key_scr[pl.ds(k0, tk), :], kpos), 1, 0).reshape(tk // 8, 8, tq)
            return acc + _multi_chain_sum([hit[g] for g in range(tk // 8)])
        acc = lax.fori_loop(0, n_chunks, body, jnp.zeros((8, tq), I32))
        return jnp.sum(acc, axis=0, keepdims=True)

    def count_ge(cand):
        return count_rows(lambda kk, kpos: kk >= cand)

    def count_half(hit_fn):
        one, zero = jnp.ones((), I16), jnp.zeros((), I16)

        def body(c, acc):
            k0 = pl.multiple_of(c * tk, tk)
            hit = jnp.where(hit_fn(half_scr[pl.ds(k0, tk), :]), one, zero).reshape(tk // 16, 16, tq)
            part = _multi_chain_sum([hit[g] for g in range(tk // 16)])
            return acc + part.astype(I32)
        acc = lax.fori_loop(0, n_chunks, body, jnp.zeros((16, tq), I32))
        return jnp.sum(acc, axis=0, keepdims=True)

    def half_search(base):
        def body(b, state):
            v, cnt = state
            cand = v + lax.shift_left(jnp.int32(1), 15 - b)
            c = base + count_half(lambda hh: hh >= cand.astype(I16))
            take = c >= k_top
            return jnp.where(take, cand, v), jnp.where(take, c, cnt)
        return lax.fori_loop(0, 16, body, (jnp.full((1, tq), I16_MIN, I32), jnp.full((1, tq), -1, I32)))

    top, cnt_top = half_search(jnp.zeros((1, tq), I32))
    above = count_half(lambda hh: hh > top.astype(I16))

    def low_body(c, carry):
        k0 = pl.multiple_of(c * tk, tk)
        kk = key_scr[pl.ds(k0, tk), :]
        low = jnp.where((kk >> 16) == top, (kk & 0xFFFF) + I16_MIN, I16_MIN)
        half_scr[pl.ds(k0, tk), :] = low.astype(I16)
        return carry

    lax.fori_loop(0, n_chunks, low_body, 0)
    low, cnt_low = half_search(above)
    thr = top * 65536 + (low - I16_MIN)
    cnt = jnp.where(cnt_low >= 0, cnt_low, cnt_top)

    @pl.when(jnp.sum(jnp.where(cnt > k_top, 1, 0)) > 0)
    def _():
        at_max = thr == INT_MAX
        above = jnp.where(at_max, 0, count_ge(jnp.where(at_max, thr, thr + 1)))
        need = k_top - above
        pos_bits = max(1, (key_scr.shape[0] - 1).bit_length())

        def pos_body(b, lo):
            cand = lo + lax.shift_left(jnp.int32(1), pos_bits - 1 - b)
            before = count_rows(lambda kk, kpos: (kk == thr) & (kpos < cand))
            return jnp.where(before < need, cand, lo)

        last = lax.fori_loop(0, pos_bits, pos_body, jnp.zeros((1, tq), I32))

        def drop_body(c, carry):
            k0 = pl.multiple_of(c * tk, tk)
            kpos = k0 + lax.broadcasted_iota(I32, (tk, 1), 0)
            kk = key_scr[pl.ds(k0, tk), :]
            key_scr[pl.ds(k0, tk), :] = jnp.where((kk == thr) & (kpos > last), INT_MIN, kk)
            return carry

        lax.fori_loop(0, n_chunks, drop_body, 0)

    thr = jnp.maximum(thr, INT_MIN + 1)

    m_scr[...] = jnp.full(m_scr.shape, NEG, F32)
    acc_scr[...] = jnp.zeros(acc_scr.shape, F32)

    def att_body(c, carry):
        k0 = pl.multiple_of(c * tk, tk)
        kv = ckv_ref[0, pl.ds(k0, tk), :]
        kv_t = ckvt_ref[0, :, pl.ds(k0, tk)]
        bias = jnp.where(key_scr[pl.ds(k0, tk), :] >= thr, 0.0, NEG)
        for j in range(N_ATTN_HEADS // hpt):
            cols = slice(j * hpt * tq, (j + 1) * hpt * tq)
            lt = jnp.dot(kv, qabst_ref[0, 0, :, cols], preferred_element_type=F32)
            ps, alphas = [], []
            for u in range(hpt):
                hc = slice((hpt * j + u) * tq, (hpt * j + u + 1) * tq)
                lg = lt[:, u * tq:(u + 1) * tq] + bias
                m_prev = m_scr[:, hc]
                m_new = jnp.maximum(m_prev, jnp.max(lg, axis=0, keepdims=True))
                ps.append(jnp.exp2(lg - m_new).astype(BF16))
                alphas.append(jnp.exp2(m_prev - m_new))
                m_scr[:, hc] = m_new
            pv = jnp.dot(kv_t, jnp.concatenate(ps, axis=1), preferred_element_type=F32)
            acc_scr[:, cols] = jnp.concatenate(alphas, axis=1) * acc_scr[:, cols] + pv
        return carry

    lax.fori_loop(0, n_chunks, att_body, 0)

    for hd in range(N_ATTN_HEADS):
        hc = slice(hd * tq, (hd + 1) * tq)
        den = acc_scr[KV_LORA:KV_LORA + 1, hc]
        o_lat = (acc_scr[0:KV_LORA, hc] * (1.0 / den)).T.astype(BF16)
        y_ref[0, :, hd * HEAD_DIM:(hd + 1) * HEAD_DIM] = jnp.dot(
            o_lat, wuv_ref[hd], preferred_element_type=F32).astype(y_ref.dtype)


def dsa_attention(q_idx_t, w_idx_t, q_abs_t, k_idx, c_kv, c_kv_t, w_uv, layer, *, tq, tk):
    b, t, _ = c_kv.shape
    k_top = min(TOPK_MAX, t // 4)
    return pl.pallas_call(
        functools.partial(_dsa_kernel, tq=tq, tk=tk, k_top=k_top),
        grid=(b, t // tq),
        in_specs=[pl.BlockSpec((1, 1, IDX_DIM, IDX_HEADS * tq), lambda bi, i: (bi, i, 0, 0)),
                  pl.BlockSpec((1, IDX_HEADS, tq), lambda bi, i: (bi, 0, i)),
                  pl.BlockSpec((1, 1, KV_LORA, N_ATTN_HEADS * tq), lambda bi, i: (bi, i, 0, 0)),
                  pl.BlockSpec((1, t, IDX_DIM), lambda bi, i: (bi, 0, 0)),
                  pl.BlockSpec((1, t, KV_LORA), lambda bi, i: (bi, 0, 0)),
                  pl.BlockSpec((1, KV_LORA + ONES_ROWS, t), lambda bi, i: (bi, 0, 0)),
                  pl.BlockSpec((None,) + w_uv.shape[1:], lambda bi, i: (layer, 0, 0, 0))],
        out_specs=pl.BlockSpec((1, tq, N_ATTN_HEADS * HEAD_DIM), lambda bi, i: (bi, i, 0)),
        out_shape=jax.ShapeDtypeStruct((b, t, N_ATTN_HEADS * HEAD_DIM), BF16),
        scratch_shapes=[pltpu.VMEM((t, tq), I32),
                        pltpu.VMEM((t, tq), I16),
                        pltpu.VMEM((1, N_ATTN_HEADS * tq), F32),
                        pltpu.VMEM((KV_LORA + ONES_ROWS, N_ATTN_HEADS * tq), F32)],
        compiler_params=_cparams("parallel", "parallel"),
        name="dsa_attention",
    )(q_idx_t, w_idx_t, q_abs_t, k_idx, c_kv, c_kv_t, w_uv)


def _pool_kernel(u_ref, halo_ref, wg_ref, sc_ref, y_ref, ext_scr, *, tm):
    i = pl.program_id(1)
    halo = halo_ref[0]
    ext_scr[0:POOL_HALO, :] = jnp.where(i > 0, halo, jnp.zeros_like(halo))
    ext_scr[POOL_HALO:POOL_HALO + tm, :] = u_ref[0]
    pos = i * tm + lax.broadcasted_iota(I32, (tm, 1), 0)
    for g, win in enumerate(POOL_WINDOWS):
        lo = g * POOL_GROUP
        tok = ext_scr[POOL_HALO:POOL_HALO + tm, lo:lo + POOL_GROUP]
        total = tok
        for k in range(1, win):
            total = total + ext_scr[POOL_HALO - k:POOL_HALO - k + tm, lo:lo + POOL_GROUP]
        cnt = jnp.minimum(pos + 1, win).astype(F32)
        mixed = (total / cnt - tok).astype(BF16)
        out = jnp.dot(mixed, wg_ref[g], preferred_element_type=F32)
        y_ref[0, :, lo:lo + POOL_GROUP] = (out * sc_ref[:, lo:lo + POOL_GROUP]).astype(y_ref.dtype)


def pool_mix(proj, w_group, scale, layer, *, tm):
    b, t, _ = proj.shape
    halo_blocks = tm // POOL_HALO
    return pl.pallas_call(
        functools.partial(_pool_kernel, tm=tm),
        grid=(b, t // tm),
        in_specs=[pl.BlockSpec((1, tm, POOL_WIDTH), lambda bi, i: (bi, i, 0)),
                  pl.BlockSpec((1, POOL_HALO, POOL_WIDTH),
                               lambda bi, i: (bi, jnp.maximum(i * halo_blocks - 1, 0), 0)),
                  pl.BlockSpec((None,) + w_group.shape[1:], lambda bi, i: (layer, 0, 0, 0)),
                  pl.BlockSpec((1, POOL_WIDTH), lambda bi, i: (0, 0))],
        out_specs=pl.BlockSpec((1, tm, POOL_WIDTH), lambda bi, i: (bi, i, 0)),
        out_shape=jax.ShapeDtypeStruct((b, t, POOL_WIDTH), BF16),
        scratch_shapes=[pltpu.VMEM((POOL_HALO + tm, POOL_WIDTH), F32)],
        compiler_params=_cparams("parallel", "parallel"),
        name="pool_mix",
    )(proj, proj, w_group, scale)


def _mix_out_kernel(y_ref, qm_ref, mkv_ref, wo_ref, x_ref, o_ref):
    qm = qm_ref[0].astype(BF16)
    mkv = mkv_ref[0]
    parts = [y_ref[0]]
    for hd in range(N_MEM_HEADS):
        lo = hd * HEAD_DIM
        logits = lax.dot_general(qm[:, lo:lo + HEAD_DIM], mkv[:, lo:lo + HEAD_DIM],
                                 (((1,), (1,)), ((), ())), preferred_element_type=F32) * HEAD_DIM ** -0.5
        p = jnp.exp(logits - jnp.max(logits, axis=-1, keepdims=True))
        den = jnp.sum(p, axis=-1, keepdims=True)
        att = jnp.dot(p.astype(BF16), mkv[:, MEM_WIDTH + lo:MEM_WIDTH + lo + HEAD_DIM],
                      preferred_element_type=F32)
        parts.append((att / den).astype(BF16))
    mix = jnp.concatenate(parts, axis=-1)
    o_ref[0] = x_ref[0] + jnp.dot(mix, wo_ref[...], preferred_element_type=F32)


def mix_out(y, q_mem, qm_block, mem_kv, w_out, layer, x, *, tm):
    b, t, d = x.shape
    return pl.pallas_call(
        _mix_out_kernel,
        grid=(b, t // tm),
        in_specs=[pl.BlockSpec((1, tm, y.shape[-1]), lambda bi, i: (bi, i, 0)),
                  pl.BlockSpec((1, tm, MEM_WIDTH), lambda bi, i: (bi, i, qm_block)),
                  pl.BlockSpec((None, 1) + mem_kv.shape[2:], lambda bi, i: (layer, bi, 0, 0)),
                  pl.BlockSpec((None,) + w_out.shape[1:], lambda bi, i: (layer, 0, 0)),
                  pl.BlockSpec((1, tm, d), lambda bi, i: (bi, i, 0))],
        out_specs=pl.BlockSpec((1, tm, d), lambda bi, i: (bi, i, 0)),
        out_shape=jax.ShapeDtypeStruct((b, t, d), F32),
        compiler_params=_cparams("parallel", "parallel"),
        name="mix_out",
    )(y, q_mem, mem_kv, w_out, x)


def _pack_attn_in(w_in):
    c_q, c_kv, k_idx, w_idx, q_mem = jnp.split(w_in, [512, 768, 832, 848], axis=-1)
    pad = jnp.zeros(w_in.shape[:-1] + (ATTN_IN_PAD - w_in.shape[-1],), w_in.dtype)
    return jnp.concatenate([c_q, c_kv, q_mem, k_idx, w_idx, pad], axis=-1)


def _pad_lanes(v, width):
    return jnp.pad(v, (0, width - v.shape[0])).reshape(1, width)


def kernel(x, mem, mixer_norm, ffn_norm, final_norm, mem_norm, w_mem_kv, w_out, w_gate, w_up, w_down,
           w_in_attn, q_norm, kv_norm, w_uq, w_uk, w_uv, w_idx_uq, idx_k_norm, idx_k_bias,
           w_in_pool, w_pool_group, pool_scale):
    b, t, d = x.shape
    depth = w_out.shape[0]
    n_mem = mem.shape[1]
    row = lambda v: v.reshape(1, -1)

    w_kv_b, w_out_b = w_mem_kv.astype(BF16), w_out.astype(BF16)
    w_gate_b, w_up_b, w_down_b = w_gate.astype(BF16), w_up.astype(BF16), w_down.astype(BF16)
    w_in_attn_b = _pack_attn_in(w_in_attn).astype(BF16)
    w_uq_t = jnp.swapaxes(w_uq, 1, 2).astype(BF16)
    w_iq_t = jnp.swapaxes(w_idx_uq, 1, 2).astype(BF16)
    w_uk_b, w_uv_b = w_uk.astype(BF16), w_uv.astype(BF16)
    w_in_pool_b, w_group_b = w_in_pool.astype(BF16), w_pool_group.astype(BF16)

    mem_kv = rms_matmul_layers(mem.reshape(b * n_mem, d), row(mem_norm), w_kv_b, tm=FFN_TM // 2, out_dtype=BF16)
    mem_kv = mem_kv.reshape(depth, b, n_mem, 2 * MEM_WIDTH)

    for i in range(depth):
        j = i // 2
        if i % 2 == 0:
            ckv, ckv_t, kidx, q_mem, q_abs_t, q_idx_t, w_idx_t = attn_proj(
                x, row(mixer_norm[i]), w_in_attn_b, row(q_norm[j]), row(kv_norm[j]),
                _pad_lanes(idx_k_norm[j], 128), _pad_lanes(idx_k_bias[j], 128), w_uq_t, w_uk_b, w_iq_t, j,
                tm=max(256, DSA_TQ), tq=DSA_TQ)
            y = dsa_attention(q_idx_t, w_idx_t, q_abs_t, kidx, ckv, ckv_t, w_uv_b, j, tq=DSA_TQ, tk=DSA_TK)
            x = mix_out(y, q_mem, 0, mem_kv, w_out_b, i, x, tm=MIX_TM)
        else:
            proj = rms_matmul(x.reshape(b * t, d), row(mixer_norm[i]), w_in_pool_b, j,
                              tm=MIX_TM, out_dtype=F32).reshape(b, t, -1)
            y = pool_mix(proj, w_group_b, row(pool_scale[j]), j, tm=MIX_TM)
            x = mix_out(y, proj, POOL_WIDTH // MEM_WIDTH, mem_kv, w_out_b, i, x, tm=MIX_TM)
        x = ffn(x.reshape(b * t, d), row(ffn_norm[i]), w_gate_b, w_up_b, w_down_b, i, row(final_norm),
                tm=FFN_TM, tf=FFN_TF, final_norm=(i == depth - 1)).reshape(b, t, d)
    return x
```

```python
import functools

import jax
import jax.numpy as jnp
from jax import lax
from jax.experimental import pallas as pl
from jax.experimental.pallas import tpu as pltpu

BF16 = jnp.bfloat16
F32 = jnp.float32
I32 = jnp.int32
I16 = jnp.int16

EPS = 1e-6
HEAD_DIM = 128
N_ATTN_HEADS = 12
Q_LORA = 512
KV_LORA = 256
IDX_HEADS = 16
IDX_DIM = 64
TOPK_MAX = 256
POOL_WINDOWS = (2, 4, 8, 16)
POOL_GROUP = 384
POOL_WIDTH = 1536
MEM_WIDTH = 512
N_MEM_HEADS = 4
POOL_HALO = 16

_CQ0, _CKV0, _QM0, _KI0, _WI0 = 0, 512, 768, 1280, 1344
ATTN_IN_PAD = 1408

LOG2E = 1.4426950408889634
INT_MIN = -(2 ** 31)
INT_MAX = 2 ** 31 - 1
I16_MIN = -(2 ** 15)
ONES_ROWS = 16
NEG = -1e30

VMEM_LIMIT = 60 * 1024 * 1024
MXU_COLS = 256
DSA_TQ = 256
DSA_TK = 512
FFN_TM, FFN_TF = 1024, 512
MIX_TM = 512


def _cparams(*sem):
    return pltpu.CompilerParams(dimension_semantics=sem, vmem_limit_bytes=VMEM_LIMIT)


def _multi_chain_sum(parts, chains=4):
    accs = list(parts[:chains])
    for n, part in enumerate(parts[chains:]):
        accs[n % chains] = accs[n % chains] + part
    while len(accs) > 1:
        accs = [a + b for a, b in zip(accs[0::2], accs[1::2])] + accs[len(accs) & ~1:]
    return accs[0]


def _loop_by_two(n, body, init):
    def two(p, carry):
        return body(2 * p + 1, body(2 * p, carry))
    carry = lax.fori_loop(0, n // 2, two, init)
    return lax.fori_loop((n // 2) * 2, n, body, carry)


def _rms(xf, g):
    ms = jnp.mean(xf * xf, axis=-1, keepdims=True)
    return xf * lax.rsqrt(ms + EPS) * g


def _rms_matmul_kernel(x_ref, g_ref, w_ref, o_ref):
    h = _rms(x_ref[...], g_ref[...]).astype(BF16)
    o_ref[...] = jnp.dot(h, w_ref[...], preferred_element_type=F32).astype(o_ref.dtype)


def rms_matmul_layers(x, g, w, *, tm, out_dtype):
    m, d = x.shape
    n_layers, _, n = w.shape
    return pl.pallas_call(
        _rms_matmul_kernel,
        grid=(m // tm, n_layers),
        in_specs=[pl.BlockSpec((tm, d), lambda i, l: (i, 0)),
                  pl.BlockSpec((1, d), lambda i, l: (0, 0)),
                  pl.BlockSpec((None, d, n), lambda i, l: (l, 0, 0))],
        out_specs=pl.BlockSpec((None, tm, n), lambda i, l: (l, i, 0)),
        out_shape=jax.ShapeDtypeStruct((n_layers, m, n), out_dtype),
        compiler_params=_cparams("parallel", "parallel"),
        name="rms_matmul_layers",
    )(x, g, w)


def _ffn_kernel(x_ref, g_ref, wg_ref, wu_ref, wd_ref, fg_ref, o_ref, h_scr, *, final_norm):
    f = pl.program_id(1)

    @pl.when(f == 0)
    def _():
        x = x_ref[...]
        h_scr[...] = _rms(x, g_ref[...]).astype(BF16)
        o_ref[...] = x

    h = h_scr[...]
    gate = jnp.dot(h, wg_ref[...], preferred_element_type=F32)
    up = jnp.dot(h, wu_ref[...], preferred_element_type=F32)
    act = (gate * jax.nn.sigmoid(gate) * up).astype(BF16)
    o_ref[...] += jnp.dot(act, wd_ref[...], preferred_element_type=F32)

    if final_norm:
        @pl.when(f == pl.num_programs(1) - 1)
        def _():
            o_ref[...] = _rms(o_ref[...], fg_ref[...])


def ffn(x, g, w_gate, w_up, w_down, layer, final_g, *, tm, tf, final_norm):
    m, d = x.shape
    dff = w_gate.shape[2]
    return pl.pallas_call(
        functools.partial(_ffn_kernel, final_norm=final_norm),
        grid=(m // tm, dff // tf),
        in_specs=[pl.BlockSpec((tm, d), lambda i, f: (i, 0)),
                  pl.BlockSpec((1, d), lambda i, f: (0, 0)),
                  pl.BlockSpec((None, d, tf), lambda i, f: (layer, 0, f)),
                  pl.BlockSpec((None, d, tf), lambda i, f: (layer, 0, f)),
                  pl.BlockSpec((None, tf, d), lambda i, f: (layer, f, 0)),
                  pl.BlockSpec((1, d), lambda i, f: (0, 0))],
        out_specs=pl.BlockSpec((tm, d), lambda i, f: (i, 0)),
        out_shape=jax.ShapeDtypeStruct((m, d), F32),
        scratch_shapes=[pltpu.VMEM((tm, d), BF16)],
        compiler_params=_cparams("parallel", "arbitrary"),
        name="ffn",
    )(x, g, w_gate, w_up, w_down, final_g)


def _attn_proj_kernel(x_ref, g_ref, win_ref, qn_ref, kvn_ref, ig_ref, ib_ref, wuqt_ref, wuk_ref, wiqt_ref,
                      ckv_ref, ckvt_ref, kidx_ref, qmem_ref, qabst_ref, qidxt_ref, widxt_ref, *, tq):
    tm = x_ref.shape[1]
    h = _rms(x_ref[0], g_ref[...]).astype(BF16)
    proj = jnp.dot(h, win_ref[...], preferred_element_type=F32)
    c_q_t = _rms(proj[:, _CQ0:_CQ0 + Q_LORA], qn_ref[...]).T.astype(BF16)
    c_kv = _rms(proj[:, _CKV0:_CKV0 + KV_LORA], kvn_ref[...])
    ckv_ref[0] = c_kv.astype(BF16)
    ckvt_ref[0, 0:KV_LORA, :] = c_kv.T.astype(BF16)
    ckvt_ref[0, KV_LORA:KV_LORA + ONES_ROWS, :] = jnp.ones((ONES_ROWS, tm), BF16)
    qmem_ref[0] = proj[:, _QM0:_QM0 + MEM_WIDTH].astype(BF16)
    slab = proj[:, _KI0:_KI0 + 128]
    live = lax.broadcasted_iota(I32, slab.shape, 1) < IDX_DIM
    mu = jnp.sum(jnp.where(live, slab, 0.0), axis=-1, keepdims=True) * (1.0 / IDX_DIM)
    cen = jnp.where(live, slab - mu, 0.0)
    var = jnp.sum(cen * cen, axis=-1, keepdims=True) * (1.0 / IDX_DIM)
    kin = cen * lax.rsqrt(var + EPS) * ig_ref[...] + ib_ref[...]
    kidx_ref[0] = kin[:, :IDX_DIM].astype(BF16)
    widxt_ref[0] = slab.T[IDX_DIM:IDX_DIM + IDX_HEADS, :] * (IDX_HEADS ** -0.5 * IDX_DIM ** -0.5)
    q_t = jnp.dot(wuqt_ref[...], c_q_t, preferred_element_type=F32).astype(BF16)
    for hd in range(N_ATTN_HEADS):
        qa_t = jnp.dot(wuk_ref[hd], q_t[hd * HEAD_DIM:(hd + 1) * HEAD_DIM, :], preferred_element_type=F32)
        qa_t = (qa_t * (HEAD_DIM ** -0.5 * LOG2E)).astype(BF16)
        for j in range(tm // tq):
            qabst_ref[0, j, :, hd * tq:(hd + 1) * tq] = qa_t[:, j * tq:(j + 1) * tq]
    qi_t = jnp.dot(wiqt_ref[...], c_q_t, preferred_element_type=F32).astype(BF16)
    for hd in range(IDX_HEADS):
        for j in range(tm // tq):
            qidxt_ref[0, j, :, hd * tq:(hd + 1) * tq] = qi_t[hd * IDX_DIM:(hd + 1) * IDX_DIM, j * tq:(j + 1) * tq]


def attn_proj(x, g, w_in, q_norm, kv_norm, idx_g, idx_b, w_uq_t, w_uk, w_iq_t, layer, *, tm, tq):
    b, t, d = x.shape
    full = lambda shape: pl.BlockSpec(shape, lambda bi, i: (0,) * len(shape))
    stacked = lambda w: pl.BlockSpec((None,) + w.shape[1:], lambda bi, i: (layer,) + (0,) * (w.ndim - 1))
    per_tile = tm // tq
    return pl.pallas_call(
        functools.partial(_attn_proj_kernel, tq=tq),
        grid=(b, t // tm),
        in_specs=[pl.BlockSpec((1, tm, d), lambda bi, i: (bi, i, 0)),
                  full((1, d)), stacked(w_in), full((1, Q_LORA)), full((1, KV_LORA)),
                  full((1, 128)), full((1, 128)), stacked(w_uq_t), stacked(w_uk), stacked(w_iq_t)],
        out_specs=[pl.BlockSpec((1, tm, KV_LORA), lambda bi, i: (bi, i, 0)),
                   pl.BlockSpec((1, KV_LORA + ONES_ROWS, tm), lambda bi, i: (bi, 0, i)),
                   pl.BlockSpec((1, tm, IDX_DIM), lambda bi, i: (bi, i, 0)),
                   pl.BlockSpec((1, tm, MEM_WIDTH), lambda bi, i: (bi, i, 0)),
                   pl.BlockSpec((1, per_tile, KV_LORA, N_ATTN_HEADS * tq), lambda bi, i: (bi, i, 0, 0)),
                   pl.BlockSpec((1, per_tile, IDX_DIM, IDX_HEADS * tq), lambda bi, i: (bi, i, 0, 0)),
                   pl.BlockSpec((1, IDX_HEADS, tm), lambda bi, i: (bi, 0, i))],
        out_shape=[jax.ShapeDtypeStruct((b, t, KV_LORA), BF16),
                   jax.ShapeDtypeStruct((b, KV_LORA + ONES_ROWS, t), BF16),
                   jax.ShapeDtypeStruct((b, t, IDX_DIM), BF16),
                   jax.ShapeDtypeStruct((b, t, MEM_WIDTH), BF16),
                   jax.ShapeDtypeStruct((b, t // tq, KV_LORA, N_ATTN_HEADS * tq), BF16),
                   jax.ShapeDtypeStruct((b, t // tq, IDX_DIM, IDX_HEADS * tq), BF16),
                   jax.ShapeDtypeStruct((b, IDX_HEADS, t), F32)],
        compiler_params=_cparams("parallel", "parallel"),
        name="attn_proj",
    )(x, g, w_in, q_norm, kv_norm, idx_g, idx_b, w_uq_t, w_uk, w_iq_t)


def _dsa_kernel(qidxt_ref, widxt_ref, qabst_ref, kidx_ref, ckv_ref, ckvt_ref, wuv_ref, y_ref,
                key_scr, half_scr, m_scr, acc_scr, *, tq, tk, k_top):
    i = pl.program_id(1)
    q0 = i * tq
    n_chunks = (q0 + tq + tk - 1) // tk
    qpos = q0 + lax.broadcasted_iota(I32, (1, tq), 1)
    hpt = max(1, MXU_COLS // tq)

    w = widxt_ref[0]

    def score_body(c, carry):
        k0 = pl.multiple_of(c * tk, tk)
        kc = kidx_ref[0, pl.ds(k0, tk), :]
        score = None
        for j in range(IDX_HEADS // hpt):
            s = jnp.dot(kc, qidxt_ref[0, 0, :, j * hpt * tq:(j + 1) * hpt * tq], preferred_element_type=F32)
            for u in range(hpt):
                hd = hpt * j + u
                term = jnp.maximum(s[:, u * tq:(u + 1) * tq], 0.0) * w[hd:hd + 1, :]
                score = term if score is None else score + term
        score = jnp.where(score == 0.0, 0.0, score)
        bits = pltpu.bitcast(score, I32)
        key = bits ^ ((bits >> 31) & 0x7FFFFFFF)
        causal = k0 + lax.broadcasted_iota(I32, (tk, 1), 0) <= qpos
        key_scr[pl.ds(k0, tk), :] = jnp.where(causal, key, INT_MIN)
        half_scr[pl.ds(k0, tk), :] = (jnp.where(causal, key, INT_MIN) >> 16).astype(I16)
        return carry

    _loop_by_two(n_chunks, score_body, 0)

    def count_rows(hit_fn):
        def body(c, acc):
            k0 = pl.multiple_of(c * tk, tk)
            kpos = k0 + lax.broadcasted_iota(I32, (tk, 1), 0)
            hit = jnp.where(hit_fn(key_scr[pl.ds(k0, tk), :], kpos), 1, 0).reshape(tk // 8, 8, tq)
            return acc + _multi_chain_sum([hit[g] for g in range(tk // 8)])
        acc = lax.fori_loop(0, n_chunks, body, jnp.zeros((8, tq), I32))
        return jnp.sum(acc, axis=0, keepdims=True)

    def count_ge(cand):
        return count_rows(lambda kk, kpos: kk >= cand)

    def count_half(hit_fn):
        one, zero = jnp.ones((), I16), jnp.zeros((), I16)

        def body(c, acc):
            k0 = pl.multiple_of(c * tk, tk)
            hit = jnp.where(hit_fn(half_scr[pl.ds(k0, tk), :]), one, zero).reshape(tk // 16, 16, tq)
            part = _multi_chain_sum([hit[g] for g in range(tk // 16)])
            return acc + part.astype(I32)
        acc = _loop_by_two(n_chunks, body, jnp.zeros((16, tq), I32))
        return jnp.sum(acc, axis=0, keepdims=True)

    def half_search(base):
        def body(b, state):
            v, cnt = state
            cand = v + lax.shift_left(jnp.int32(1), 15 - b)
            c = base + count_half(lambda hh: hh >= cand.astype(I16))
            take = c >= k_top
            return jnp.where(take, cand, v), jnp.where(take, c, cnt)
        return lax.fori_loop(0, 16, body, (jnp.full((1, tq), I16_MIN, I32), jnp.full((1, tq), -1, I32)))

    top, cnt_top = half_search(jnp.zeros((1, tq), I32))
    above = count_half(lambda hh: hh > top.astype(I16))

    def low_body(c, carry):
        k0 = pl.multiple_of(c * tk, tk)
        kk = key_scr[pl.ds(k0, tk), :]
        low = jnp.where((kk >> 16) == top, (kk & 0xFFFF) + I16_MIN, I16_MIN)
        half_scr[pl.ds(k0, tk), :] = low.astype(I16)
        return carry

    lax.fori_loop(0, n_chunks, low_body, 0)
    low, cnt_low = half_search(above)
    thr = top * 65536 + (low - I16_MIN)
    cnt = jnp.where(cnt_low >= 0, cnt_low, cnt_top)

    @pl.when(jnp.sum(jnp.where(cnt > k_top, 1, 0)) > 0)
    def _():
        at_max = thr == INT_MAX
        above = jnp.where(at_max, 0, count_ge(jnp.where(at_max, thr, thr + 1)))
        need = k_top - above
        pos_bits = max(1, (key_scr.shape[0] - 1).bit_length())

        def pos_body(b, lo):
            cand = lo + lax.shift_left(jnp.int32(1), pos_bits - 1 - b)
            before = count_rows(lambda kk, kpos: (kk == thr) & (kpos < cand))
            return jnp.where(before < need, cand, lo)

        last = lax.fori_loop(0, pos_bits, pos_body, jnp.zeros((1, tq), I32))

        def drop_body(c, carry):
            k0 = pl.multiple_of(c * tk, tk)
            kpos = k0 + lax.broadcasted_iota(I32, (tk, 1), 0)
            kk = key_scr[pl.ds(k0, tk), :]
            key_scr[pl.ds(k0, tk), :] = jnp.where((kk == thr) & (kpos > last), INT_MIN, kk)
            return carry

        lax.fori_loop(0, n_chunks, drop_body, 0)

    thr = jnp.maximum(thr, INT_MIN + 1)

    m_scr[...] = jnp.full(m_scr.shape, NEG, F32)
    acc_scr[...] = jnp.zeros(acc_scr.shape, F32)

    def att_body(c, carry):
        k0 = pl.multiple_of(c * tk, tk)
        kv = ckv_ref[0, pl.ds(k0, tk), :]
        kv_t = ckvt_ref[0, :, pl.ds(k0, tk)]
        bias = jnp.where(key_scr[pl.ds(k0, tk), :] >= thr, 0.0, NEG)
        for j in range(N_ATTN_HEADS // hpt):
            cols = slice(j * hpt * tq, (j + 1) * hpt * tq)
            lt = jnp.dot(kv, qabst_ref[0, 0, :, cols], preferred_element_type=F32)
            ps, alphas = [], []
            for u in range(hpt):
                hc = slice((hpt * j + u) * tq, (hpt * j + u + 1) * tq)
                lg = lt[:, u * tq:(u + 1) * tq] + bias
                m_prev = m_scr[:, hc]
                m_new = jnp.maximum(m_prev, jnp.max(lg, axis=0, keepdims=True))
                ps.append(jnp.exp2(lg - m_new).astype(BF16))
                alphas.append(jnp.exp2(m_prev - m_new))
                m_scr[:, hc] = m_new
            pv = jnp.dot(kv_t, jnp.concatenate(ps, axis=1), preferred_element_type=F32)
            acc_scr[:, cols] = jnp.concatenate(alphas, axis=1) * acc_scr[:, cols] + pv
        return carry

    lax.fori_loop(0, n_chunks, att_body, 0)

    for hd in range(N_ATTN_HEADS):
        hc = slice(hd * tq, (hd + 1) * tq)
        den = acc_scr[KV_LORA:KV_LORA + 1, hc]
        o_lat = (acc_scr[0:KV_LORA, hc] * (1.0 / den)).T.astype(BF16)
        y_ref[0, :, hd * HEAD_DIM:(hd + 1) * HEAD_DIM] = jnp.dot(
            o_lat, wuv_ref[hd], preferred_element_type=F32).astype(y_ref.dtype)


def dsa_attention(q_idx_t, w_idx_t, q_abs_t, k_idx, c_kv, c_kv_t, w_uv, layer, *, tq, tk):
    b, t, _ = c_kv.shape
    k_top = min(TOPK_MAX, t // 4)
    return pl.pallas_call(
        functools.partial(_dsa_kernel, tq=tq, tk=tk, k_top=k_top),
        grid=(b, t // tq),
        in_specs=[pl.BlockSpec((1, 1, IDX_DIM, IDX_HEADS * tq), lambda bi, i: (bi, i, 0, 0)),
                  pl.BlockSpec((1, IDX_HEADS, tq), lambda bi, i: (bi, 0, i)),
                  pl.BlockSpec((1, 1, KV_LORA, N_ATTN_HEADS * tq), lambda bi, i: (bi, i, 0, 0)),
                  pl.BlockSpec((1, t, IDX_DIM), lambda bi, i: (bi, 0, 0)),
                  pl.BlockSpec((1, t, KV_LORA), lambda bi, i: (bi, 0, 0)),
                  pl.BlockSpec((1, KV_LORA + ONES_ROWS, t), lambda bi, i: (bi, 0, 0)),
                  pl.BlockSpec((None,) + w_uv.shape[1:], lambda bi, i: (layer, 0, 0, 0))],
        out_specs=pl.BlockSpec((1, tq, N_ATTN_HEADS * HEAD_DIM), lambda bi, i: (bi, i, 0)),
        out_shape=jax.ShapeDtypeStruct((b, t, N_ATTN_HEADS * HEAD_DIM), BF16),
        scratch_shapes=[pltpu.VMEM((t, tq), I32),
                        pltpu.VMEM((t, tq), I16),
                        pltpu.VMEM((1, N_ATTN_HEADS * tq), F32),
                        pltpu.VMEM((KV_LORA + ONES_ROWS, N_ATTN_HEADS * tq), F32)],
        compiler_params=_cparams("parallel", "parallel"),
        name="dsa_attention",
    )(q_idx_t, w_idx_t, q_abs_t, k_idx, c_kv, c_kv_t, w_uv)


def _pool_kernel(x_ref, xh_ref, g_ref, win_ref, wg_ref, sc_ref, y_ref, qm_ref, ext_scr, *, tm):
    i = pl.program_id(1)
    rows = jnp.concatenate([xh_ref[0], x_ref[0]], axis=0)
    proj = jnp.dot(_rms(rows, g_ref[...]).astype(BF16), win_ref[...], preferred_element_type=F32)
    qm_ref[0] = proj[POOL_HALO:, POOL_WIDTH:].astype(qm_ref.dtype)
    ext_scr[...] = proj[:, :POOL_WIDTH]

    @pl.when(i == 0)
    def _():
        ext_scr[0:POOL_HALO, :] = jnp.zeros((POOL_HALO, POOL_WIDTH), F32)

    pos = i * tm + lax.broadcasted_iota(I32, (tm, 1), 0)
    for g, win in enumerate(POOL_WINDOWS):
        lo = g * POOL_GROUP
        tok = ext_scr[POOL_HALO:POOL_HALO + tm, lo:lo + POOL_GROUP]
        total = tok
        for k in range(1, win):
            total = total + ext_scr[POOL_HALO - k:POOL_HALO - k + tm, lo:lo + POOL_GROUP]
        cnt = jnp.minimum(pos + 1, win).astype(F32)
        mixed = (total / cnt - tok).astype(BF16)
        out = jnp.dot(mixed, wg_ref[g], preferred_element_type=F32)
        y_ref[0, :, lo:lo + POOL_GROUP] = (out * sc_ref[:, lo:lo + POOL_GROUP]).astype(y_ref.dtype)


def pool_layer(x, g, w_in, w_group, scale, layer, *, tm):
    b, t, d = x.shape
    halo_blocks = tm // POOL_HALO
    return pl.pallas_call(
        functools.partial(_pool_kernel, tm=tm),
        grid=(b, t // tm),
        in_specs=[pl.BlockSpec((1, tm, d), lambda bi, i: (bi, i, 0)),
                  pl.BlockSpec((1, POOL_HALO, d), lambda bi, i: (bi, jnp.maximum(i * halo_blocks - 1, 0), 0)),
                  pl.BlockSpec((1, d), lambda bi, i: (0, 0)),
                  pl.BlockSpec((None,) + w_in.shape[1:], lambda bi, i: (layer, 0, 0)),
                  pl.BlockSpec((None,) + w_group.shape[1:], lambda bi, i: (layer, 0, 0, 0)),
                  pl.BlockSpec((1, POOL_WIDTH), lambda bi, i: (0, 0))],
        out_specs=[pl.BlockSpec((1, tm, POOL_WIDTH), lambda bi, i: (bi, i, 0)),
                   pl.BlockSpec((1, tm, MEM_WIDTH), lambda bi, i: (bi, i, 0))],
        out_shape=[jax.ShapeDtypeStruct((b, t, POOL_WIDTH), BF16),
                   jax.ShapeDtypeStruct((b, t, MEM_WIDTH), BF16)],
        scratch_shapes=[pltpu.VMEM((POOL_HALO + tm, POOL_WIDTH), F32)],
        compiler_params=_cparams("parallel", "parallel"),
        name="pool_layer",
    )(x, x, g, w_in, w_group, scale)


def _mix_out_kernel(y_ref, qm_ref, mkv_ref, wo_ref, x_ref, o_ref):
    qm = qm_ref[0]
    mkv = mkv_ref[0]
    parts = [y_ref[0]]
    for hd in range(N_MEM_HEADS):
        lo = hd * HEAD_DIM
        logits = lax.dot_general(qm[:, lo:lo + HEAD_DIM], mkv[:, lo:lo + HEAD_DIM],
                                 (((1,), (1,)), ((), ())), preferred_element_type=F32) * HEAD_DIM ** -0.5
        p = jnp.exp(logits - jnp.max(logits, axis=-1, keepdims=True))
        den = jnp.sum(p, axis=-1, keepdims=True)
        att = jnp.dot(p.astype(BF16), mkv[:, MEM_WIDTH + lo:MEM_WIDTH + lo + HEAD_DIM],
                      preferred_element_type=F32)
        parts.append((att / den).astype(BF16))
    mix = jnp.concatenate(parts, axis=-1)
    o_ref[0] = x_ref[0] + jnp.dot(mix, wo_ref[...], preferred_element_type=F32)


def mix_out(y, q_mem, mem_kv, w_out, layer, x, *, tm):
    b, t, d = x.shape
    return pl.pallas_call(
        _mix_out_kernel,
        grid=(b, t // tm),
        in_specs=[pl.BlockSpec((1, tm, y.shape[-1]), lambda bi, i: (bi, i, 0)),
                  pl.BlockSpec((1, tm, MEM_WIDTH), lambda bi, i: (bi, i, 0)),
                  pl.BlockSpec((None, 1) + mem_kv.shape[2:], lambda bi, i: (layer, bi, 0, 0)),
                  pl.BlockSpec((None,) + w_out.shape[1:], lambda bi, i: (layer, 0, 0)),
                  pl.BlockSpec((1, tm, d), lambda bi, i: (bi, i, 0))],
        out_specs=pl.BlockSpec((1, tm, d), lambda bi, i: (bi, i, 0)),
        out_shape=jax.ShapeDtypeStruct((b, t, d), F32),
        compiler_params=_cparams("parallel", "parallel"),
        name="mix_out",
    )(y, q_mem, mem_kv, w_out, x)


def _pack_attn_in(w_in):
    c_q, c_kv, k_idx, w_idx, q_mem = jnp.split(w_in, [512, 768, 832, 848], axis=-1)
    pad = jnp.zeros(w_in.shape[:-1] + (ATTN_IN_PAD - w_in.shape[-1],), w_in.dtype)
    return jnp.concatenate([c_q, c_kv, q_mem, k_idx, w_idx, pad], axis=-1)


def _pad_lanes(v, width):
    return jnp.pad(v, (0, width - v.shape[0])).reshape(1, width)


def kernel(x, mem, mixer_norm, ffn_norm, final_norm, mem_norm, w_mem_kv, w_out, w_gate, w_up, w_down,
           w_in_attn, q_norm, kv_norm, w_uq, w_uk, w_uv, w_idx_uq, idx_k_norm, idx_k_bias,
           w_in_pool, w_pool_group, pool_scale):
    b, t, d = x.shape
    depth = w_out.shape[0]
    n_mem = mem.shape[1]
    row = lambda v: v.reshape(1, -1)

    w_kv_b, w_out_b = w_mem_kv.astype(BF16), w_out.astype(BF16)
    w_gate_b, w_up_b, w_down_b = w_gate.astype(BF16), w_up.astype(BF16), w_down.astype(BF16)
    w_in_attn_b = _pack_attn_in(w_in_attn).astype(BF16)
    w_uq_t = jnp.swapaxes(w_uq, 1, 2).astype(BF16)
    w_iq_t = jnp.swapaxes(w_idx_uq, 1, 2).astype(BF16)
    w_uk_b, w_uv_b = w_uk.astype(BF16), w_uv.astype(BF16)
    w_in_pool_b, w_group_b = w_in_pool.astype(BF16), w_pool_group.astype(BF16)

    mem_kv = rms_matmul_layers(mem.reshape(b * n_mem, d), row(mem_norm), w_kv_b, tm=FFN_TM // 2, out_dtype=BF16)
    mem_kv = mem_kv.reshape(depth, b, n_mem, 2 * MEM_WIDTH)

    for i in range(depth):
        j = i // 2
        if i % 2 == 0:
            ckv, ckv_t, kidx, q_mem, q_abs_t, q_idx_t, w_idx_t = attn_proj(
                x, row(mixer_norm[i]), w_in_attn_b, row(q_norm[j]), row(kv_norm[j]),
                _pad_lanes(idx_k_norm[j], 128), _pad_lanes(idx_k_bias[j], 128), w_uq_t, w_uk_b, w_iq_t, j,
                tm=max(256, DSA_TQ), tq=DSA_TQ)
            y = dsa_attention(q_idx_t, w_idx_t, q_abs_t, kidx, ckv, ckv_t, w_uv_b, j, tq=DSA_TQ, tk=DSA_TK)
            x = mix_out(y, q_mem, mem_kv, w_out_b, i, x, tm=MIX_TM)
        else:
            y, q_mem = pool_layer(x, row(mixer_norm[i]), w_in_pool_b, w_group_b, row(pool_scale[j]), j, tm=MIX_TM)
            x = mix_out(y, q_mem, mem_kv, w_out_b, i, x, tm=MIX_TM)
        x = ffn(x.reshape(b * t, d), row(ffn_norm[i]), w_gate_b, w_up_b, w_down_b, i, row(final_norm),
                tm=FFN_TM, tf=FFN_TF, final_norm=(i == depth - 1)).reshape(b, t, d)
    return x
```

```python
import functools

import jax
import jax.numpy as jnp
from jax import lax
from jax.experimental import pallas as pl
from jax.experimental.pallas import tpu as pltpu

BF16 = jnp.bfloat16
F32 = jnp.float32
I32 = jnp.int32
I16 = jnp.int16

EPS = 1e-6
HEAD_DIM = 128
N_ATTN_HEADS = 12
Q_LORA = 512
KV_LORA = 256
IDX_HEADS = 16
IDX_DIM = 64
TOPK_MAX = 256
POOL_WINDOWS = (2, 4, 8, 16)
POOL_GROUP = 384
POOL_WIDTH = 1536
MEM_WIDTH = 512
N_MEM_HEADS = 4
POOL_HALO = 32
SUBLANES = 8

_CQ0, _CKV0, _QM0, _KI0, _WI0 = 0, 512, 768, 1280, 1344
ATTN_IN_PAD = 1408

LOG2E = 1.4426950408889634
INT_MIN = -(2 ** 31)
INT_MAX = 2 ** 31 - 1
I16_MIN = -(2 ** 15)
ONES_ROWS = 16
NEG = -1e30

VMEM_LIMIT = 60 * 1024 * 1024
MXU_COLS = 256
DSA_TQ = 256
DSA_TK = 512
FFN_TM, FFN_TF = 1024, 512
MIX_TM = 512


def _cparams(*sem):
    return pltpu.CompilerParams(dimension_semantics=sem, vmem_limit_bytes=VMEM_LIMIT)


def _multi_chain_sum(parts, chains=4):
    accs = list(parts[:chains])
    for n, part in enumerate(parts[chains:]):
        accs[n % chains] = accs[n % chains] + part
    while len(accs) > 1:
        accs = [a + b for a, b in zip(accs[0::2], accs[1::2])] + accs[len(accs) & ~1:]
    return accs[0]


def _loop_by_two(n, body, init):
    def two(p, carry):
        return body(2 * p + 1, body(2 * p, carry))
    carry = lax.fori_loop(0, n // 2, two, init)
    return lax.fori_loop((n // 2) * 2, n, body, carry)


def _rms(xf, g):
    ms = jnp.mean(xf * xf, axis=-1, keepdims=True)
    return xf * lax.rsqrt(ms + EPS) * g


def _rms_matmul_kernel(x_ref, g_ref, w_ref, o_ref):
    h = _rms(x_ref[...], g_ref[...]).astype(BF16)
    o_ref[...] = jnp.dot(h, w_ref[...], preferred_element_type=F32).astype(o_ref.dtype)


def rms_matmul_layers(x, g, w, *, tm, out_dtype):
    m, d = x.shape
    n_layers, _, n = w.shape
    return pl.pallas_call(
        _rms_matmul_kernel,
        grid=(m // tm, n_layers),
        in_specs=[pl.BlockSpec((tm, d), lambda i, l: (i, 0)),
                  pl.BlockSpec((1, d), lambda i, l: (0, 0)),
                  pl.BlockSpec((None, d, n), lambda i, l: (l, 0, 0))],
        out_specs=pl.BlockSpec((None, tm, n), lambda i, l: (l, i, 0)),
        out_shape=jax.ShapeDtypeStruct((n_layers, m, n), out_dtype),
        compiler_params=_cparams("parallel", "parallel"),
        name="rms_matmul_layers",
    )(x, g, w)


def _ffn_kernel(x_ref, g_ref, wg_ref, wu_ref, wd_ref, fg_ref, o_ref, h_scr, *, final_norm):
    f = pl.program_id(1)

    @pl.when(f == 0)
    def _():
        x = x_ref[...]
        h_scr[...] = _rms(x, g_ref[...]).astype(BF16)
        o_ref[...] = x

    h = h_scr[...]
    gate = jnp.dot(h, wg_ref[...], preferred_element_type=F32)
    up = jnp.dot(h, wu_ref[...], preferred_element_type=F32)
    act = (gate * jax.nn.sigmoid(gate) * up).astype(BF16)
    o_ref[...] += jnp.dot(act, wd_ref[...], preferred_element_type=F32)

    if final_norm:
        @pl.when(f == pl.num_programs(1) - 1)
        def _():
            o_ref[...] = _rms(o_ref[...], fg_ref[...])


def ffn(x, g, w_gate, w_up, w_down, layer, final_g, *, tm, tf, final_norm):
    m, d = x.shape
    dff = w_gate.shape[2]
    return pl.pallas_call(
        functools.partial(_ffn_kernel, final_norm=final_norm),
        grid=(m // tm, dff // tf),
        in_specs=[pl.BlockSpec((tm, d), lambda i, f: (i, 0)),
                  pl.BlockSpec((1, d), lambda i, f: (0, 0)),
                  pl.BlockSpec((None, d, tf), lambda i, f: (layer, 0, f)),
                  pl.BlockSpec((None, d, tf), lambda i, f: (layer, 0, f)),
                  pl.BlockSpec((None, tf, d), lambda i, f: (layer, f, 0)),
                  pl.BlockSpec((1, d), lambda i, f: (0, 0))],
        out_specs=pl.BlockSpec((tm, d), lambda i, f: (i, 0)),
        out_shape=jax.ShapeDtypeStruct((m, d), F32),
        scratch_shapes=[pltpu.VMEM((tm, d), BF16)],
        compiler_params=_cparams("parallel", "arbitrary"),
        name="ffn",
    )(x, g, w_gate, w_up, w_down, final_g)


def _attn_proj_kernel(x_ref, g_ref, win_ref, qn_ref, kvn_ref, ig_ref, ib_ref, wuqt_ref, wuk_ref, wiqt_ref,
                      ckv_ref, ckvt_ref, kidx_ref, qmem_ref, qabst_ref, qidxt_ref, widxt_ref, *, tq):
    tm = x_ref.shape[1]
    h = _rms(x_ref[0], g_ref[...]).astype(BF16)
    proj = jnp.dot(h, win_ref[...], preferred_element_type=F32)
    c_q_t = _rms(proj[:, _CQ0:_CQ0 + Q_LORA], qn_ref[...]).T.astype(BF16)
    c_kv = _rms(proj[:, _CKV0:_CKV0 + KV_LORA], kvn_ref[...])
    ckv_ref[0] = c_kv.astype(BF16)
    ckvt_ref[0, 0:KV_LORA, :] = c_kv.T.astype(BF16)
    ckvt_ref[0, KV_LORA:KV_LORA + ONES_ROWS, :] = jnp.ones((ONES_ROWS, tm), BF16)
    qmem_ref[0] = proj[:, _QM0:_QM0 + MEM_WIDTH].astype(BF16)
    slab = proj[:, _KI0:_KI0 + 128]
    live = lax.broadcasted_iota(I32, slab.shape, 1) < IDX_DIM
    mu = jnp.sum(jnp.where(live, slab, 0.0), axis=-1, keepdims=True) * (1.0 / IDX_DIM)
    cen = jnp.where(live, slab - mu, 0.0)
    var = jnp.sum(cen * cen, axis=-1, keepdims=True) * (1.0 / IDX_DIM)
    kin = cen * lax.rsqrt(var + EPS) * ig_ref[...] + ib_ref[...]
    kidx_ref[0] = kin[:, :IDX_DIM].astype(BF16)
    widxt_ref[0] = slab.T[IDX_DIM:IDX_DIM + IDX_HEADS, :] * (IDX_HEADS ** -0.5 * IDX_DIM ** -0.5)
    q_t = jnp.dot(wuqt_ref[...], c_q_t, preferred_element_type=F32).astype(BF16)
    for hd in range(N_ATTN_HEADS):
        qa_t = jnp.dot(wuk_ref[hd], q_t[hd * HEAD_DIM:(hd + 1) * HEAD_DIM, :], preferred_element_type=F32)
        qa_t = (qa_t * (HEAD_DIM ** -0.5 * LOG2E)).astype(BF16)
        for j in range(tm // tq):
            qabst_ref[0, j, :, hd * tq:(hd + 1) * tq] = qa_t[:, j * tq:(j + 1) * tq]
    qi_t = jnp.dot(wiqt_ref[...], c_q_t, preferred_element_type=F32).astype(BF16)
    for hd in range(IDX_HEADS):
        for j in range(tm // tq):
            qidxt_ref[0, j, :, hd * tq:(hd + 1) * tq] = qi_t[hd * IDX_DIM:(hd + 1) * IDX_DIM, j * tq:(j + 1) * tq]


def attn_proj(x, g, w_in, q_norm, kv_norm, idx_g, idx_b, w_uq_t, w_uk, w_iq_t, layer, *, tm, tq):
    b, t, d = x.shape
    full = lambda shape: pl.BlockSpec(shape, lambda bi, i: (0,) * len(shape))
    stacked = lambda w: pl.BlockSpec((None,) + w.shape[1:], lambda bi, i: (layer,) + (0,) * (w.ndim - 1))
    per_tile = tm // tq
    return pl.pallas_call(
        functools.partial(_attn_proj_kernel, tq=tq),
        grid=(b, t // tm),
        in_specs=[pl.BlockSpec((1, tm, d), lambda bi, i: (bi, i, 0)),
                  full((1, d)), stacked(w_in), full((1, Q_LORA)), full((1, KV_LORA)),
                  full((1, 128)), full((1, 128)), stacked(w_uq_t), stacked(w_uk), stacked(w_iq_t)],
        out_specs=[pl.BlockSpec((1, tm, KV_LORA), lambda bi, i: (bi, i, 0)),
                   pl.BlockSpec((1, KV_LORA + ONES_ROWS, tm), lambda bi, i: (bi, 0, i)),
                   pl.BlockSpec((1, tm, IDX_DIM), lambda bi, i: (bi, i, 0)),
                   pl.BlockSpec((1, tm, MEM_WIDTH), lambda bi, i: (bi, i, 0)),
                   pl.BlockSpec((1, per_tile, KV_LORA, N_ATTN_HEADS * tq), lambda bi, i: (bi, i, 0, 0)),
                   pl.BlockSpec((1, per_tile, IDX_DIM, IDX_HEADS * tq), lambda bi, i: (bi, i, 0, 0)),
                   pl.BlockSpec((1, IDX_HEADS, tm), lambda bi, i: (bi, 0, i))],
        out_shape=[jax.ShapeDtypeStruct((b, t, KV_LORA), BF16),
                   jax.ShapeDtypeStruct((b, KV_LORA + ONES_ROWS, t), BF16),
                   jax.ShapeDtypeStruct((b, t, IDX_DIM), BF16),
                   jax.ShapeDtypeStruct((b, t, MEM_WIDTH), BF16),
                   jax.ShapeDtypeStruct((b, t // tq, KV_LORA, N_ATTN_HEADS * tq), BF16),
                   jax.ShapeDtypeStruct((b, t // tq, IDX_DIM, IDX_HEADS * tq), BF16),
                   jax.ShapeDtypeStruct((b, IDX_HEADS, t), F32)],
        compiler_params=_cparams("parallel", "parallel"),
        name="attn_proj",
    )(x, g, w_in, q_norm, kv_norm, idx_g, idx_b, w_uq_t, w_uk, w_iq_t)


def _dsa_kernel(qidxt_ref, widxt_ref, qabst_ref, kidx_ref, ckv_ref, ckvt_ref, wuv_ref, y_ref,
                key_scr, half_scr, m_scr, acc_scr, *, tq, tk, k_top):
    i = pl.program_id(1)
    q0 = i * tq
    n_chunks = (q0 + tq + tk - 1) // tk
    qpos = q0 + lax.broadcasted_iota(I32, (1, tq), 1)
    hpt = max(1, MXU_COLS // tq)

    w = widxt_ref[0]

    def score_body(c, carry):
        k0 = pl.multiple_of(c * tk, tk)
        kc = kidx_ref[0, pl.ds(k0, tk), :]
        score = None
        for j in range(IDX_HEADS // hpt):
            s = jnp.dot(kc, qidxt_ref[0, 0, :, j * hpt * tq:(j + 1) * hpt * tq], preferred_element_type=F32)
            for u in range(hpt):
                hd = hpt * j + u
                term = jnp.maximum(s[:, u * tq:(u + 1) * tq], 0.0) * w[hd:hd + 1, :]
                score = term if score is None else score + term
        score = jnp.where(score == 0.0, 0.0, score)
        bits = pltpu.bitcast(score, I32)
        key = bits ^ ((bits >> 31) & 0x7FFFFFFF)
        causal = k0 + lax.broadcasted_iota(I32, (tk, 1), 0) <= qpos
        key_scr[pl.ds(k0, tk), :] = jnp.where(causal, key, INT_MIN)
        half_scr[pl.ds(k0, tk), :] = (jnp.where(causal, key, INT_MIN) >> 16).astype(I16)
        return carry

    _loop_by_two(n_chunks, score_body, 0)

    def count_rows(hit_fn):
        def body(c, acc):
            k0 = pl.multiple_of(c * tk, tk)
            kpos = k0 + lax.broadcasted_iota(I32, (tk, 1), 0)
            hit = jnp.where(hit_fn(key_scr[pl.ds(k0, tk), :], kpos), 1, 0).reshape(tk // 8, 8, tq)
            return acc + _multi_chain_sum([hit[g] for g in range(tk // 8)])
        acc = lax.fori_loop(0, n_chunks, body, jnp.zeros((8, tq), I32))
        return jnp.sum(acc, axis=0, keepdims=True)

    def count_ge(cand):
        return count_rows(lambda kk, kpos: kk >= cand)

    def count_half(hit_fn):
        one, zero = jnp.ones((), I16), jnp.zeros((), I16)

        def body(c, acc):
            k0 = pl.multiple_of(c * tk, tk)
            hit = jnp.where(hit_fn(half_scr[pl.ds(k0, tk), :]), one, zero).reshape(tk // 16, 16, tq)
            part = _multi_chain_sum([hit[g] for g in range(tk // 16)])
            return acc + part.astype(I32)
        acc = _loop_by_two(n_chunks, body, jnp.zeros((16, tq), I32))
        return jnp.sum(acc, axis=0, keepdims=True)

    def half_search(base):
        def body(b, state):
            v, cnt = state
            cand = v + lax.shift_left(jnp.int32(1), 15 - b)
            c = base + count_half(lambda hh: hh >= cand.astype(I16))
            take = c >= k_top
            return jnp.where(take, cand, v), jnp.where(take, c, cnt)
        return lax.fori_loop(0, 16, body, (jnp.full((1, tq), I16_MIN, I32), jnp.full((1, tq), -1, I32)))

    top, cnt_top = half_search(jnp.zeros((1, tq), I32))
    above = count_half(lambda hh: hh > top.astype(I16))

    def low_body(c, carry):
        k0 = pl.multiple_of(c * tk, tk)
        kk = key_scr[pl.ds(k0, tk), :]
        low = jnp.where((kk >> 16) == top, (kk & 0xFFFF) + I16_MIN, I16_MIN)
        half_scr[pl.ds(k0, tk), :] = low.astype(I16)
        return carry

    lax.fori_loop(0, n_chunks, low_body, 0)
    low, cnt_low = half_search(above)
    thr = top * 65536 + (low - I16_MIN)
    cnt = jnp.where(cnt_low >= 0, cnt_low, cnt_top)

    @pl.when(jnp.sum(jnp.where(cnt > k_top, 1, 0)) > 0)
    def _():
        at_max = thr == INT_MAX
        above = jnp.where(at_max, 0, count_ge(jnp.where(at_max, thr, thr + 1)))
        need = k_top - above
        pos_bits = max(1, (key_scr.shape[0] - 1).bit_length())

        def pos_body(b, lo):
            cand = lo + lax.shift_left(jnp.int32(1), pos_bits - 1 - b)
            before = count_rows(lambda kk, kpos: (kk == thr) & (kpos < cand))
            return jnp.where(before < need, cand, lo)

        last = lax.fori_loop(0, pos_bits, pos_body, jnp.zeros((1, tq), I32))

        def drop_body(c, carry):
            k0 = pl.multiple_of(c * tk, tk)
            kpos = k0 + lax.broadcasted_iota(I32, (tk, 1), 0)
            kk = key_scr[pl.ds(k0, tk), :]
            key_scr[pl.ds(k0, tk), :] = jnp.where((kk == thr) & (kpos > last), INT_MIN, kk)
            return carry

        lax.fori_loop(0, n_chunks, drop_body, 0)

    thr = jnp.maximum(thr, INT_MIN + 1)

    m_scr[...] = jnp.full(m_scr.shape, NEG, F32)
    acc_scr[...] = jnp.zeros(acc_scr.shape, F32)

    n_tiles = N_ATTN_HEADS // hpt
    tile_w = hpt * tq

    def softmax_tile(j, kv, bias):
        lt = jnp.dot(kv, qabst_ref[0, 0, :, j * tile_w:(j + 1) * tile_w], preferred_element_type=F32)
        ps, alphas = [], []
        for u in range(hpt):
            hc = slice((hpt * j + u) * tq, (hpt * j + u + 1) * tq)
            lg = lt[:, u * tq:(u + 1) * tq] + bias
            m_prev = m_scr[:, hc]
            m_new = jnp.maximum(m_prev, jnp.max(lg, axis=0, keepdims=True))
            ps.append(jnp.exp2(lg - m_new).astype(BF16))
            alphas.append(jnp.exp2(m_prev - m_new))
            m_scr[:, hc] = m_new
        return jnp.concatenate(ps, axis=1), jnp.concatenate(alphas, axis=1)

    def accumulate(j, kv_t, p, alpha):
        cols = slice(j * tile_w, (j + 1) * tile_w)
        acc_scr[:, cols] = alpha * acc_scr[:, cols] + jnp.dot(kv_t, p, preferred_element_type=F32)

    def att_body(c, carry):
        k0 = pl.multiple_of(c * tk, tk)
        kv = ckv_ref[0, pl.ds(k0, tk), :]
        kv_t = ckvt_ref[0, :, pl.ds(k0, tk)]
        bias = jnp.where(key_scr[pl.ds(k0, tk), :] >= thr, 0.0, NEG)
        for j in range(n_tiles):
            accumulate(j, kv_t, *softmax_tile(j, kv, bias))
        return carry

    lax.fori_loop(0, n_chunks, att_body, 0)

    for hd in range(N_ATTN_HEADS):
        hc = slice(hd * tq, (hd + 1) * tq)
        den = acc_scr[KV_LORA:KV_LORA + 1, hc]
        o_lat = (acc_scr[0:KV_LORA, hc] * (1.0 / den)).T.astype(BF16)
        y_ref[0, :, hd * HEAD_DIM:(hd + 1) * HEAD_DIM] = jnp.dot(
            o_lat, wuv_ref[hd], preferred_element_type=F32).astype(y_ref.dtype)


def dsa_attention(q_idx_t, w_idx_t, q_abs_t, k_idx, c_kv, c_kv_t, w_uv, layer, *, tq, tk):
    b, t, _ = c_kv.shape
    k_top = min(TOPK_MAX, t // 4)
    return pl.pallas_call(
        functools.partial(_dsa_kernel, tq=tq, tk=tk, k_top=k_top),
        grid=(b, t // tq),
        in_specs=[pl.BlockSpec((1, 1, IDX_DIM, IDX_HEADS * tq), lambda bi, i: (bi, i, 0, 0)),
                  pl.BlockSpec((1, IDX_HEADS, tq), lambda bi, i: (bi, 0, i)),
                  pl.BlockSpec((1, 1, KV_LORA, N_ATTN_HEADS * tq), lambda bi, i: (bi, i, 0, 0)),
                  pl.BlockSpec((1, t, IDX_DIM), lambda bi, i: (bi, 0, 0)),
                  pl.BlockSpec((1, t, KV_LORA), lambda bi, i: (bi, 0, 0)),
                  pl.BlockSpec((1, KV_LORA + ONES_ROWS, t), lambda bi, i: (bi, 0, 0)),
                  pl.BlockSpec((None,) + w_uv.shape[1:], lambda bi, i: (layer, 0, 0, 0))],
        out_specs=pl.BlockSpec((1, tq, N_ATTN_HEADS * HEAD_DIM), lambda bi, i: (bi, i, 0)),
        out_shape=jax.ShapeDtypeStruct((b, t, N_ATTN_HEADS * HEAD_DIM), BF16),
        scratch_shapes=[pltpu.VMEM((t, tq), I32),
                        pltpu.VMEM((t, tq), I16),
                        pltpu.VMEM((1, N_ATTN_HEADS * tq), F32),
                        pltpu.VMEM((KV_LORA + ONES_ROWS, N_ATTN_HEADS * tq), F32)],
        compiler_params=_cparams("parallel", "parallel"),
        name="dsa_attention",
    )(q_idx_t, w_idx_t, q_abs_t, k_idx, c_kv, c_kv_t, w_uv)


def _pool_kernel(x_ref, xh_ref, g_ref, win_ref, wg_ref, sc_ref, y_ref, qm_ref, ext_scr, a_scr, b_scr, *, tm):
    i = pl.program_id(1)
    rows = jnp.concatenate([xh_ref[0], x_ref[0]], axis=0)
    proj = jnp.dot(_rms(rows, g_ref[...]).astype(BF16), win_ref[...], preferred_element_type=F32)
    qm_ref[0] = proj[POOL_HALO:, POOL_WIDTH:].astype(qm_ref.dtype)
    ext_scr[...] = proj[:, :POOL_WIDTH]

    @pl.when(i == 0)
    def _():
        ext_scr[0:POOL_HALO, :] = jnp.zeros((POOL_HALO, POOL_WIDTH), F32)

    n_rows = POOL_HALO + tm
    pos = i * tm + lax.broadcasted_iota(I32, (tm, 1), 0)
    for g, win in enumerate(POOL_WINDOWS):
        lo = g * POOL_GROUP
        tok = ext_scr[POOL_HALO:POOL_HALO + tm, lo:lo + POOL_GROUP]
        src, width, start = ext_scr.at[:, lo:lo + POOL_GROUP], 1, 0
        for dst in (a_scr, b_scr, a_scr)[:max(0, win.bit_length() - 2)]:
            start += SUBLANES
            dst[start:, :] = src[start:, :] + src[start - width:n_rows - width, :]
            src, width = dst, 2 * width
        total = src[POOL_HALO:, :] + src[POOL_HALO - width:n_rows - width, :]
        cnt = jnp.minimum(pos + 1, win).astype(F32)
        mixed = (total / cnt - tok).astype(BF16)
        out = jnp.dot(mixed, wg_ref[g], preferred_element_type=F32)
        y_ref[0, :, lo:lo + POOL_GROUP] = (out * sc_ref[:, lo:lo + POOL_GROUP]).astype(y_ref.dtype)


def pool_layer(x, g, w_in, w_group, scale, layer, *, tm):
    b, t, d = x.shape
    halo_blocks = tm // POOL_HALO
    return pl.pallas_call(
        functools.partial(_pool_kernel, tm=tm),
        grid=(b, t // tm),
        in_specs=[pl.BlockSpec((1, tm, d), lambda bi, i: (bi, i, 0)),
                  pl.BlockSpec((1, POOL_HALO, d), lambda bi, i: (bi, jnp.maximum(i * halo_blocks - 1, 0), 0)),
                  pl.BlockSpec((1, d), lambda bi, i: (0, 0)),
                  pl.BlockSpec((None,) + w_in.shape[1:], lambda bi, i: (layer, 0, 0)),
                  pl.BlockSpec((None,) + w_group.shape[1:], lambda bi, i: (layer, 0, 0, 0)),
                  pl.BlockSpec((1, POOL_WIDTH), lambda bi, i: (0, 0))],
        out_specs=[pl.BlockSpec((1, tm, POOL_WIDTH), lambda bi, i: (bi, i, 0)),
                   pl.BlockSpec((1, tm, MEM_WIDTH), lambda bi, i: (bi, i, 0))],
        out_shape=[jax.ShapeDtypeStruct((b, t, POOL_WIDTH), BF16),
                   jax.ShapeDtypeStruct((b, t, MEM_WIDTH), BF16)],
        scratch_shapes=[pltpu.VMEM((POOL_HALO + tm, POOL_WIDTH), F32),
                        pltpu.VMEM((POOL_HALO + tm, POOL_GROUP), F32),
                        pltpu.VMEM((POOL_HALO + tm, POOL_GROUP), F32)],
        compiler_params=_cparams("parallel", "parallel"),
        name="pool_layer",
    )(x, x, g, w_in, w_group, scale)


def _mix_out_kernel(y_ref, qm_ref, mkv_ref, wo_ref, x_ref, o_ref):
    qm = qm_ref[0]
    mkv = mkv_ref[0]
    parts = [y_ref[0]]
    for hd in range(N_MEM_HEADS):
        lo = hd * HEAD_DIM
        logits = lax.dot_general(qm[:, lo:lo + HEAD_DIM], mkv[:, lo:lo + HEAD_DIM],
                                 (((1,), (1,)), ((), ())), preferred_element_type=F32) * HEAD_DIM ** -0.5
        p = jnp.exp(logits - jnp.max(logits, axis=-1, keepdims=True))
        den = jnp.sum(p, axis=-1, keepdims=True)
        att = jnp.dot(p.astype(BF16), mkv[:, MEM_WIDTH + lo:MEM_WIDTH + lo + HEAD_DIM],
                      preferred_element_type=F32)
        parts.append((att / den).astype(BF16))
    mix = jnp.concatenate(parts, axis=-1)
    o_ref[0] = x_ref[0] + jnp.dot(mix, wo_ref[...], preferred_element_type=F32)


def mix_out(y, q_mem, mem_kv, w_out, layer, x, *, tm):
    b, t, d = x.shape
    return pl.pallas_call(
        _mix_out_kernel,
        grid=(b, t // tm),
        in_specs=[pl.BlockSpec((1, tm, y.shape[-1]), lambda bi, i: (bi, i, 0)),
                  pl.BlockSpec((1, tm, MEM_WIDTH), lambda bi, i: (bi, i, 0)),
                  pl.BlockSpec((None, 1) + mem_kv.shape[2:], lambda bi, i: (layer, bi, 0, 0)),
                  pl.BlockSpec((None,) + w_out.shape[1:], lambda bi, i: (layer, 0, 0)),
                  pl.BlockSpec((1, tm, d), lambda bi, i: (bi, i, 0))],
        out_specs=pl.BlockSpec((1, tm, d), lambda bi, i: (bi, i, 0)),
        out_shape=jax.ShapeDtypeStruct((b, t, d), F32),
        compiler_params=_cparams("parallel", "parallel"),
        name="mix_out",
    )(y, q_mem, mem_kv, w_out, x)


def _pack_attn_in(w_in):
    c_q, c_kv, k_idx, w_idx, q_mem = jnp.split(w_in, [512, 768, 832, 848], axis=-1)
    pad = jnp.zeros(w_in.shape[:-1] + (ATTN_IN_PAD - w_in.shape[-1],), w_in.dtype)
    return jnp.concatenate([c_q, c_kv, q_mem, k_idx, w_idx, pad], axis=-1)


def _pad_lanes(v, width):
    return jnp.pad(v, (0, width - v.shape[0])).reshape(1, width)


def kernel(x, mem, mixer_norm, ffn_norm, final_norm, mem_norm, w_mem_kv, w_out, w_gate, w_up, w_down,
           w_in_attn, q_norm, kv_norm, w_uq, w_uk, w_uv, w_idx_uq, idx_k_norm, idx_k_bias,
           w_in_pool, w_pool_group, pool_scale):
    b, t, d = x.shape
    depth = w_out.shape[0]
    n_mem = mem.shape[1]
    row = lambda v: v.reshape(1, -1)

    w_kv_b, w_out_b = w_mem_kv.astype(BF16), w_out.astype(BF16)
    w_gate_b, w_up_b, w_down_b = w_gate.astype(BF16), w_up.astype(BF16), w_down.astype(BF16)
    w_in_attn_b = _pack_attn_in(w_in_attn).astype(BF16)
    w_uq_t = jnp.swapaxes(w_uq, 1, 2).astype(BF16)
    w_iq_t = jnp.swapaxes(w_idx_uq, 1, 2).astype(BF16)
    w_uk_b, w_uv_b = w_uk.astype(BF16), w_uv.astype(BF16)
    w_in_pool_b, w_group_b = w_in_pool.astype(BF16), w_pool_group.astype(BF16)

    mem_kv = rms_matmul_layers(mem.reshape(b * n_mem, d), row(mem_norm), w_kv_b, tm=FFN_TM // 2, out_dtype=BF16)
    mem_kv = mem_kv.reshape(depth, b, n_mem, 2 * MEM_WIDTH)

    for i in range(depth):
        j = i // 2
        if i % 2 == 0:
            ckv, ckv_t, kidx, q_mem, q_abs_t, q_idx_t, w_idx_t = attn_proj(
                x, row(mixer_norm[i]), w_in_attn_b, row(q_norm[j]), row(kv_norm[j]),
                _pad_lanes(idx_k_norm[j], 128), _pad_lanes(idx_k_bias[j], 128), w_uq_t, w_uk_b, w_iq_t, j,
                tm=max(MIX_TM, DSA_TQ), tq=DSA_TQ)
            y = dsa_attention(q_idx_t, w_idx_t, q_abs_t, kidx, ckv, ckv_t, w_uv_b, j, tq=DSA_TQ, tk=DSA_TK)
            x = mix_out(y, q_mem, mem_kv, w_out_b, i, x, tm=MIX_TM)
        else:
            y, q_mem = pool_layer(x, row(mixer_norm[i]), w_in_pool_b, w_group_b, row(pool_scale[j]), j, tm=MIX_TM)
            x = mix_out(y, q_mem, mem_kv, w_out_b, i, x, tm=MIX_TM)
        x = ffn(x.reshape(b * t, d), row(ffn_norm[i]), w_gate_b, w_up_b, w_down_b, i, row(final_norm),
                tm=FFN_TM, tf=FFN_TF, final_norm=(i == depth - 1)).reshape(b, t, d)
    return x
```

```python
import functools

import jax
import jax.numpy as jnp
from jax import lax
from jax.experimental import pallas as pl
from jax.experimental.pallas import tpu as pltpu

BF16 = jnp.bfloat16
F32 = jnp.float32
I32 = jnp.int32
I16 = jnp.int16

EPS = 1e-6
HEAD_DIM = 128
N_ATTN_HEADS = 12
Q_LORA = 512
KV_LORA = 256
IDX_HEADS = 16
IDX_DIM = 64
TOPK_MAX = 256
POOL_WINDOWS = (2, 4, 8, 16)
POOL_GROUP = 384
POOL_WIDTH = 1536
MEM_WIDTH = 512
N_MEM_HEADS = 4
POOL_HALO = 32
SUBLANES = 8

LANES = 128
_CQ0, _CKV0, _QM0, _KI0 = 0, Q_LORA, Q_LORA + KV_LORA, Q_LORA + KV_LORA + MEM_WIDTH
ATTN_IN_PAD = _KI0 + LANES

LOG2E = 1.4426950408889634
INT_MIN = -(2 ** 31)
INT_MAX = 2 ** 31 - 1
I16_MIN = -(2 ** 15)
ONES_ROWS = 16
NEG = -1e30

VMEM_LIMIT = 60 * 1024 * 1024
MXU_COLS = 256
DSA_TQ = 256
DSA_TK = 512
FFN_TM, FFN_TF = 1024, 512
MIX_TM = 512


def _cparams(*sem):
    return pltpu.CompilerParams(dimension_semantics=sem, vmem_limit_bytes=VMEM_LIMIT)


def _multi_chain_sum(parts, chains=4):
    accs = list(parts[:chains])
    for n, part in enumerate(parts[chains:]):
        accs[n % chains] = accs[n % chains] + part
    while len(accs) > 1:
        accs = [a + b for a, b in zip(accs[0::2], accs[1::2])] + accs[len(accs) & ~1:]
    return accs[0]


def _loop_by_two(n, body, init):
    def two(p, carry):
        return body(2 * p + 1, body(2 * p, carry))
    carry = lax.fori_loop(0, n // 2, two, init)
    return lax.fori_loop((n // 2) * 2, n, body, carry)


def _rms(xf, g):
    ms = jnp.mean(xf * xf, axis=-1, keepdims=True)
    return xf * lax.rsqrt(ms + EPS) * g


def _rms_matmul_kernel(x_ref, g_ref, w_ref, o_ref):
    h = _rms(x_ref[...], g_ref[...]).astype(BF16)
    o_ref[...] = jnp.dot(h, w_ref[...], preferred_element_type=F32).astype(o_ref.dtype)


def rms_matmul_layers(x, g, w, *, tm, out_dtype):
    m, d = x.shape
    n_layers, _, n = w.shape
    return pl.pallas_call(
        _rms_matmul_kernel,
        grid=(m // tm, n_layers),
        in_specs=[pl.BlockSpec((tm, d), lambda i, l: (i, 0)),
                  pl.BlockSpec((1, d), lambda i, l: (0, 0)),
                  pl.BlockSpec((None, d, n), lambda i, l: (l, 0, 0))],
        out_specs=pl.BlockSpec((None, tm, n), lambda i, l: (l, i, 0)),
        out_shape=jax.ShapeDtypeStruct((n_layers, m, n), out_dtype),
        compiler_params=_cparams("parallel", "parallel"),
        name="rms_matmul_layers",
    )(x, g, w)


def _ffn_kernel(x_ref, g_ref, wg_ref, wu_ref, wd_ref, fg_ref, *rest, final_norm, cast_next):
    if cast_next:
        ng_ref, nu_ref, nd_ref, o_ref, cg_ref, cu_ref, cd_ref, h_scr = rest
        cg_ref[...] = ng_ref[...].astype(BF16)
        cu_ref[...] = nu_ref[...].astype(BF16)
        cd_ref[...] = nd_ref[...].astype(BF16)
    else:
        o_ref, h_scr = rest
    f = pl.program_id(1)

    @pl.when(f == 0)
    def _():
        x = x_ref[...]
        h_scr[...] = _rms(x, g_ref[...]).astype(BF16)
        o_ref[...] = x

    h = h_scr[...]
    gate = jnp.dot(h, wg_ref[...], preferred_element_type=F32)
    up = jnp.dot(h, wu_ref[...], preferred_element_type=F32)
    act = (gate * jax.nn.sigmoid(gate) * up).astype(BF16)
    o_ref[...] += jnp.dot(act, wd_ref[...], preferred_element_type=F32)

    if final_norm:
        @pl.when(f == pl.num_programs(1) - 1)
        def _():
            o_ref[...] = _rms(o_ref[...], fg_ref[...])


def ffn(x, g, w_gate, w_up, w_down, final_g, next_f32, *, tm, tf, final_norm):
    m, d = x.shape
    dff = w_gate.shape[1]
    nm, nf = m // tm, dff // tf
    in_specs = [pl.BlockSpec((tm, d), lambda i, f: (i, 0)),
                pl.BlockSpec((1, d), lambda i, f: (0, 0)),
                pl.BlockSpec((d, tf), lambda i, f: (0, f)),
                pl.BlockSpec((d, tf), lambda i, f: (0, f)),
                pl.BlockSpec((tf, d), lambda i, f: (f, 0)),
                pl.BlockSpec((1, d), lambda i, f: (0, 0))]
    out_specs = [pl.BlockSpec((tm, d), lambda i, f: (i, 0))]
    out_shape = [jax.ShapeDtypeStruct((m, d), F32)]
    args = [x, g, w_gate, w_up, w_down, final_g]
    if next_f32 is not None:
        *stacked, layer = next_f32
        assert d % nm == 0 and (d // nm) % (2 * SUBLANES) == 0
        ds = d // nm
        in_specs += [pl.BlockSpec((None, ds, tf), lambda i, f: (layer, i, f)),
                     pl.BlockSpec((None, ds, tf), lambda i, f: (layer, i, f)),
                     pl.BlockSpec((None, tf, ds), lambda i, f: (layer, f, i))]
        out_specs += [pl.BlockSpec((ds, tf), lambda i, f: (i, f)),
                      pl.BlockSpec((ds, tf), lambda i, f: (i, f)),
                      pl.BlockSpec((tf, ds), lambda i, f: (f, i))]
        out_shape += [jax.ShapeDtypeStruct(w.shape[1:], BF16) for w in stacked]
        args += stacked
    out, *casted = pl.pallas_call(
        functools.partial(_ffn_kernel, final_norm=final_norm, cast_next=next_f32 is not None),
        grid=(nm, nf),
        in_specs=in_specs,
        out_specs=out_specs,
        out_shape=out_shape,
        scratch_shapes=[pltpu.VMEM((tm, d), BF16)],
        compiler_params=_cparams("parallel", "arbitrary"),
        name="ffn",
    )(*args)
    return out, casted


def _attn_proj_kernel(x_ref, g_ref, win_ref, qn_ref, kvn_ref, ig_ref, ib_ref, wuqt_ref, wuk_ref, wiqt_ref,
                      ckv_ref, ckvt_ref, kidx_ref, qmem_ref, qabst_ref, qidxt_ref, widxt_ref, *, tq):
    tm = x_ref.shape[1]
    h = _rms(x_ref[0], g_ref[...]).astype(BF16)
    proj = jnp.dot(h, win_ref[...], preferred_element_type=F32)
    c_q_t = _rms(proj[:, _CQ0:_CQ0 + Q_LORA], qn_ref[...]).T.astype(BF16)
    c_kv = _rms(proj[:, _CKV0:_CKV0 + KV_LORA], kvn_ref[...])
    ckv_ref[0] = c_kv.astype(BF16)
    ckvt_ref[0, 0:KV_LORA, :] = c_kv.T.astype(BF16)
    ckvt_ref[0, KV_LORA:KV_LORA + ONES_ROWS, :] = jnp.ones((ONES_ROWS, tm), BF16)
    qmem_ref[0] = proj[:, _QM0:_QM0 + MEM_WIDTH].astype(BF16)
    slab = proj[:, _KI0:_KI0 + LANES]
    live = lax.broadcasted_iota(I32, slab.shape, 1) < IDX_DIM
    mu = jnp.sum(jnp.where(live, slab, 0.0), axis=-1, keepdims=True) * (1.0 / IDX_DIM)
    cen = jnp.where(live, slab - mu, 0.0)
    var = jnp.sum(cen * cen, axis=-1, keepdims=True) * (1.0 / IDX_DIM)
    kin = cen * lax.rsqrt(var + EPS) * ig_ref[...] + ib_ref[...]
    kidx_ref[0] = kin[:, :IDX_DIM].astype(BF16)
    widxt_ref[0] = slab.T[IDX_DIM:IDX_DIM + IDX_HEADS, :] * (IDX_HEADS ** -0.5 * IDX_DIM ** -0.5)
    q_t = jnp.dot(wuqt_ref[...], c_q_t, preferred_element_type=F32).astype(BF16)
    for hd in range(N_ATTN_HEADS):
        qa_t = jnp.dot(wuk_ref[hd], q_t[hd * HEAD_DIM:(hd + 1) * HEAD_DIM, :], preferred_element_type=F32)
        qa_t = (qa_t * (HEAD_DIM ** -0.5 * LOG2E)).astype(BF16)
        for j in range(tm // tq):
            qabst_ref[0, j, :, hd * tq:(hd + 1) * tq] = qa_t[:, j * tq:(j + 1) * tq]
    qi_t = jnp.dot(wiqt_ref[...], c_q_t, preferred_element_type=F32).astype(BF16)
    for hd in range(IDX_HEADS):
        for j in range(tm // tq):
            qidxt_ref[0, j, :, hd * tq:(hd + 1) * tq] = qi_t[hd * IDX_DIM:(hd + 1) * IDX_DIM, j * tq:(j + 1) * tq]


def attn_proj(x, g, w_in, q_norm, kv_norm, idx_g, idx_b, w_uq_t, w_uk, w_iq_t, layer, *, tm, tq):
    b, t, d = x.shape
    full = lambda shape: pl.BlockSpec(shape, lambda bi, i: (0,) * len(shape))
    stacked = lambda w: pl.BlockSpec((None,) + w.shape[1:], lambda bi, i: (layer,) + (0,) * (w.ndim - 1))
    per_tile = tm // tq
    return pl.pallas_call(
        functools.partial(_attn_proj_kernel, tq=tq),
        grid=(b, t // tm),
        in_specs=[pl.BlockSpec((1, tm, d), lambda bi, i: (bi, i, 0)),
                  full((1, d)), stacked(w_in), full((1, Q_LORA)), full((1, KV_LORA)),
                  full((1, LANES)), full((1, LANES)), stacked(w_uq_t), stacked(w_uk), stacked(w_iq_t)],
        out_specs=[pl.BlockSpec((1, tm, KV_LORA), lambda bi, i: (bi, i, 0)),
                   pl.BlockSpec((1, KV_LORA + ONES_ROWS, tm), lambda bi, i: (bi, 0, i)),
                   pl.BlockSpec((1, tm, IDX_DIM), lambda bi, i: (bi, i, 0)),
                   pl.BlockSpec((1, tm, MEM_WIDTH), lambda bi, i: (bi, i, 0)),
                   pl.BlockSpec((1, per_tile, KV_LORA, N_ATTN_HEADS * tq), lambda bi, i: (bi, i, 0, 0)),
                   pl.BlockSpec((1, per_tile, IDX_DIM, IDX_HEADS * tq), lambda bi, i: (bi, i, 0, 0)),
                   pl.BlockSpec((1, IDX_HEADS, tm), lambda bi, i: (bi, 0, i))],
        out_shape=[jax.ShapeDtypeStruct((b, t, KV_LORA), BF16),
                   jax.ShapeDtypeStruct((b, KV_LORA + ONES_ROWS, t), BF16),
                   jax.ShapeDtypeStruct((b, t, IDX_DIM), BF16),
                   jax.ShapeDtypeStruct((b, t, MEM_WIDTH), BF16),
                   jax.ShapeDtypeStruct((b, t // tq, KV_LORA, N_ATTN_HEADS * tq), BF16),
                   jax.ShapeDtypeStruct((b, t // tq, IDX_DIM, IDX_HEADS * tq), BF16),
                   jax.ShapeDtypeStruct((b, IDX_HEADS, t), F32)],
        compiler_params=_cparams("parallel", "parallel"),
        name="attn_proj",
    )(x, g, w_in, q_norm, kv_norm, idx_g, idx_b, w_uq_t, w_uk, w_iq_t)


def _dsa_kernel(qidxt_ref, widxt_ref, qabst_ref, kidx_ref, ckv_ref, ckvt_ref, wuv_ref, y_ref,
                key_scr, half_scr, m_scr, acc_scr, *, tq, tk, k_top):
    i = pl.program_id(1)
    q0 = i * tq
    n_chunks = (q0 + tq + tk - 1) // tk
    qpos = q0 + lax.broadcasted_iota(I32, (1, tq), 1)
    hpt = max(1, MXU_COLS // tq)

    w = widxt_ref[0]

    def score_body(c, carry):
        k0 = pl.multiple_of(c * tk, tk)
        kc = kidx_ref[0, pl.ds(k0, tk), :]
        score = None
        for j in range(IDX_HEADS // hpt):
            s = jnp.dot(kc, qidxt_ref[0, 0, :, j * hpt * tq:(j + 1) * hpt * tq], preferred_element_type=F32)
            for u in range(hpt):
                hd = hpt * j + u
                term = jnp.maximum(s[:, u * tq:(u + 1) * tq], 0.0) * w[hd:hd + 1, :]
                score = term if score is None else score + term
        score = jnp.where(score == 0.0, 0.0, score)
        bits = pltpu.bitcast(score, I32)
        key = bits ^ ((bits >> 31) & 0x7FFFFFFF)
        causal = k0 + lax.broadcasted_iota(I32, (tk, 1), 0) <= qpos
        key_scr[pl.ds(k0, tk), :] = jnp.where(causal, key, INT_MIN)
        half_scr[pl.ds(k0, tk), :] = (jnp.where(causal, key, INT_MIN) >> 16).astype(I16)
        return carry

    _loop_by_two(n_chunks, score_body, 0)

    def count_rows(hit_fn):
        def body(c, acc):
            k0 = pl.multiple_of(c * tk, tk)
            kpos = k0 + lax.broadcasted_iota(I32, (tk, 1), 0)
            hit = jnp.where(hit_fn(key_scr[pl.ds(k0, tk), :], kpos), 1, 0).reshape(tk // 8, 8, tq)
            return acc + _multi_chain_sum([hit[g] for g in range(tk // 8)])
        acc = lax.fori_loop(0, n_chunks, body, jnp.zeros((8, tq), I32))
        return jnp.sum(acc, axis=0, keepdims=True)

    def count_ge(cand):
        return count_rows(lambda kk, kpos: kk >= cand)

    def count_half(hit_fn):
        one, zero = jnp.ones((), I16), jnp.zeros((), I16)

        def body(c, acc):
            k0 = pl.multiple_of(c * tk, tk)
            hit = jnp.where(hit_fn(half_scr[pl.ds(k0, tk), :]), one, zero).reshape(tk // 16, 16, tq)
            part = _multi_chain_sum([hit[g] for g in range(tk // 16)])
            return acc + part.astype(I32)
        acc = _loop_by_two(n_chunks, body, jnp.zeros((16, tq), I32))
        return jnp.sum(acc, axis=0, keepdims=True)

    def half_search(base):
        def body(b, state):
            v, cnt = state
            cand = v + lax.shift_left(jnp.int32(1), 15 - b)
            c = base + count_half(lambda hh: hh >= cand.astype(I16))
            take = c >= k_top
            return jnp.where(take, cand, v), jnp.where(take, c, cnt)
        return lax.fori_loop(0, 16, body, (jnp.full((1, tq), I16_MIN, I32), jnp.full((1, tq), -1, I32)))

    top, cnt_top = half_search(jnp.zeros((1, tq), I32))
    above = count_half(lambda hh: hh > top.astype(I16))

    def low_body(c, carry):
        k0 = pl.multiple_of(c * tk, tk)
        kk = key_scr[pl.ds(k0, tk), :]
        low = jnp.where((kk >> 16) == top, (kk & 0xFFFF) + I16_MIN, I16_MIN)
        half_scr[pl.ds(k0, tk), :] = low.astype(I16)
        return carry

    lax.fori_loop(0, n_chunks, low_body, 0)
    low, cnt_low = half_search(above)
    thr = top * 65536 + (low - I16_MIN)
    cnt = jnp.where(cnt_low >= 0, cnt_low, cnt_top)

    @pl.when(jnp.sum(jnp.where(cnt > k_top, 1, 0)) > 0)
    def _():
        at_max = thr == INT_MAX
        above = jnp.where(at_max, 0, count_ge(jnp.where(at_max, thr, thr + 1)))
        need = k_top - above
        pos_bits = max(1, (key_scr.shape[0] - 1).bit_length())

        def pos_body(b, lo):
            cand = lo + lax.shift_left(jnp.int32(1), pos_bits - 1 - b)
            before = count_rows(lambda kk, kpos: (kk == thr) & (kpos < cand))
            return jnp.where(before < need, cand, lo)

        last = lax.fori_loop(0, pos_bits, pos_body, jnp.zeros((1, tq), I32))

        def drop_body(c, carry):
            k0 = pl.multiple_of(c * tk, tk)
            kpos = k0 + lax.broadcasted_iota(I32, (tk, 1), 0)
            kk = key_scr[pl.ds(k0, tk), :]
            key_scr[pl.ds(k0, tk), :] = jnp.where((kk == thr) & (kpos > last), INT_MIN, kk)
            return carry

        lax.fori_loop(0, n_chunks, drop_body, 0)

    thr = jnp.maximum(thr, INT_MIN + 1)

    m_scr[...] = jnp.full(m_scr.shape, NEG, F32)
    acc_scr[...] = jnp.zeros(acc_scr.shape, F32)

    n_tiles = N_ATTN_HEADS // hpt
    tile_w = hpt * tq

    def softmax_tile(j, kv, bias):
        lt = jnp.dot(kv, qabst_ref[0, 0, :, j * tile_w:(j + 1) * tile_w], preferred_element_type=F32)
        ps, alphas = [], []
        for u in range(hpt):
            hc = slice((hpt * j + u) * tq, (hpt * j + u + 1) * tq)
            lg = lt[:, u * tq:(u + 1) * tq] + bias
            m_prev = m_scr[:, hc]
            m_new = jnp.maximum(m_prev, jnp.max(lg, axis=0, keepdims=True))
            ps.append(jnp.exp2(lg - m_new).astype(BF16))
            alphas.append(jnp.exp2(m_prev - m_new))
            m_scr[:, hc] = m_new
        return jnp.concatenate(ps, axis=1), jnp.concatenate(alphas, axis=1)

    def accumulate(j, kv_t, p, alpha):
        cols = slice(j * tile_w, (j + 1) * tile_w)
        acc_scr[:, cols] = alpha * acc_scr[:, cols] + jnp.dot(kv_t, p, preferred_element_type=F32)

    def att_body(c, carry):
        k0 = pl.multiple_of(c * tk, tk)
        kv = ckv_ref[0, pl.ds(k0, tk), :]
        kv_t = ckvt_ref[0, :, pl.ds(k0, tk)]
        bias = jnp.where(key_scr[pl.ds(k0, tk), :] >= thr, 0.0, NEG)
        for j in range(n_tiles):
            accumulate(j, kv_t, *softmax_tile(j, kv, bias))
        return carry

    lax.fori_loop(0, n_chunks, att_body, 0)

    for hd in range(N_ATTN_HEADS):
        hc = slice(hd * tq, (hd + 1) * tq)
        den = acc_scr[KV_LORA:KV_LORA + 1, hc]
        o_lat = (acc_scr[0:KV_LORA, hc] * (1.0 / den)).T.astype(BF16)
        y_ref[0, :, hd * HEAD_DIM:(hd + 1) * HEAD_DIM] = jnp.dot(
            o_lat, wuv_ref[hd], preferred_element_type=F32).astype(y_ref.dtype)


def dsa_attention(q_idx_t, w_idx_t, q_abs_t, k_idx, c_kv, c_kv_t, w_uv, layer, *, tq, tk):
    b, t, _ = c_kv.shape
    k_top = min(TOPK_MAX, t // 4)
    return pl.pallas_call(
        functools.partial(_dsa_kernel, tq=tq, tk=tk, k_top=k_top),
        grid=(b, t // tq),
        in_specs=[pl.BlockSpec((1, 1, IDX_DIM, IDX_HEADS * tq), lambda bi, i: (bi, i, 0, 0)),
                  pl.BlockSpec((1, IDX_HEADS, tq), lambda bi, i: (bi, 0, i)),
                  pl.BlockSpec((1, 1, KV_LORA, N_ATTN_HEADS * tq), lambda bi, i: (bi, i, 0, 0)),
                  pl.BlockSpec((1, t, IDX_DIM), lambda bi, i: (bi, 0, 0)),
                  pl.BlockSpec((1, t, KV_LORA), lambda bi, i: (bi, 0, 0)),
                  pl.BlockSpec((1, KV_LORA + ONES_ROWS, t), lambda bi, i: (bi, 0, 0)),
                  pl.BlockSpec((None,) + w_uv.shape[1:], lambda bi, i: (layer, 0, 0, 0))],
        out_specs=pl.BlockSpec((1, tq, N_ATTN_HEADS * HEAD_DIM), lambda bi, i: (bi, i, 0)),
        out_shape=jax.ShapeDtypeStruct((b, t, N_ATTN_HEADS * HEAD_DIM), BF16),
        scratch_shapes=[pltpu.VMEM((t, tq), I32),
                        pltpu.VMEM((t, tq), I16),
                        pltpu.VMEM((1, N_ATTN_HEADS * tq), F32),
                        pltpu.VMEM((KV_LORA + ONES_ROWS, N_ATTN_HEADS * tq), F32)],
        compiler_params=_cparams("parallel", "parallel"),
        name="dsa_attention",
    )(q_idx_t, w_idx_t, q_abs_t, k_idx, c_kv, c_kv_t, w_uv)


def _pool_kernel(x_ref, xh_ref, g_ref, win_ref, wg_ref, sc_ref, y_ref, qm_ref, ext_scr, a_scr, b_scr, *, tm):
    i = pl.program_id(1)
    rows = jnp.concatenate([xh_ref[0], x_ref[0]], axis=0)
    proj = jnp.dot(_rms(rows, g_ref[...]).astype(BF16), win_ref[...], preferred_element_type=F32)
    qm_ref[0] = proj[POOL_HALO:, POOL_WIDTH:].astype(qm_ref.dtype)
    ext_scr[...] = proj[:, :POOL_WIDTH]

    @pl.when(i == 0)
    def _():
        ext_scr[0:POOL_HALO, :] = jnp.zeros((POOL_HALO, POOL_WIDTH), F32)

    n_rows = POOL_HALO + tm
    pos = i * tm + lax.broadcasted_iota(I32, (tm, 1), 0)
    for g, win in enumerate(POOL_WINDOWS):
        lo = g * POOL_GROUP
        tok = ext_scr[POOL_HALO:POOL_HALO + tm, lo:lo + POOL_GROUP]
        src, width, start = ext_scr.at[:, lo:lo + POOL_GROUP], 1, 0
        for dst in (a_scr, b_scr, a_scr)[:max(0, win.bit_length() - 2)]:
            start += SUBLANES
            dst[start:, :] = src[start:, :] + src[start - width:n_rows - width, :]
            src, width = dst, 2 * width
        total = src[POOL_HALO:, :] + src[POOL_HALO - width:n_rows - width, :]
        cnt = jnp.minimum(pos + 1, win).astype(F32)
        mixed = (total / cnt - tok).astype(BF16)
        out = jnp.dot(mixed, wg_ref[g], preferred_element_type=F32)
        y_ref[0, :, lo:lo + POOL_GROUP] = (out * sc_ref[:, lo:lo + POOL_GROUP]).astype(y_ref.dtype)


def pool_layer(x, g, w_in, w_group, scale, layer, *, tm):
    b, t, d = x.shape
    halo_blocks = tm // POOL_HALO
    return pl.pallas_call(
        functools.partial(_pool_kernel, tm=tm),
        grid=(b, t // tm),
        in_specs=[pl.BlockSpec((1, tm, d), lambda bi, i: (bi, i, 0)),
                  pl.BlockSpec((1, POOL_HALO, d), lambda bi, i: (bi, jnp.maximum(i * halo_blocks - 1, 0), 0)),
                  pl.BlockSpec((1, d), lambda bi, i: (0, 0)),
                  pl.BlockSpec((None,) + w_in.shape[1:], lambda bi, i: (layer, 0, 0)),
                  pl.BlockSpec((None,) + w_group.shape[1:], lambda bi, i: (layer, 0, 0, 0)),
                  pl.BlockSpec((1, POOL_WIDTH), lambda bi, i: (0, 0))],
        out_specs=[pl.BlockSpec((1, tm, POOL_WIDTH), lambda bi, i: (bi, i, 0)),
                   pl.BlockSpec((1, tm, MEM_WIDTH), lambda bi, i: (bi, i, 0))],
        out_shape=[jax.ShapeDtypeStruct((b, t, POOL_WIDTH), BF16),
                   jax.ShapeDtypeStruct((b, t, MEM_WIDTH), BF16)],
        scratch_shapes=[pltpu.VMEM((POOL_HALO + tm, POOL_WIDTH), F32),
                        pltpu.VMEM((POOL_HALO + tm, POOL_GROUP), F32),
                        pltpu.VMEM((POOL_HALO + tm, POOL_GROUP), F32)],
        compiler_params=_cparams("parallel", "parallel"),
        name="pool_layer",
    )(x, x, g, w_in, w_group, scale)


def _mix_out_kernel(y_ref, qm_ref, mkv_ref, wo_ref, x_ref, o_ref):
    qm = qm_ref[0]
    mkv = mkv_ref[0]
    parts = [y_ref[0]]
    for hd in range(N_MEM_HEADS):
        lo = hd * HEAD_DIM
        logits = lax.dot_general(qm[:, lo:lo + HEAD_DIM], mkv[:, lo:lo + HEAD_DIM],
                                 (((1,), (1,)), ((), ())), preferred_element_type=F32) * HEAD_DIM ** -0.5
        p = jnp.exp(logits - jnp.max(logits, axis=-1, keepdims=True))
        den = jnp.sum(p, axis=-1, keepdims=True)
        att = jnp.dot(p.astype(BF16), mkv[:, MEM_WIDTH + lo:MEM_WIDTH + lo + HEAD_DIM],
                      preferred_element_type=F32)
        parts.append((att / den).astype(BF16))
    mix = jnp.concatenate(parts, axis=-1)
    o_ref[0] = x_ref[0] + jnp.dot(mix, wo_ref[...], preferred_element_type=F32)


def mix_out(y, q_mem, mem_kv, w_out, layer, x, *, tm):
    b, t, d = x.shape
    return pl.pallas_call(
        _mix_out_kernel,
        grid=(b, t // tm),
        in_specs=[pl.BlockSpec((1, tm, y.shape[-1]), lambda bi, i: (bi, i, 0)),
                  pl.BlockSpec((1, tm, MEM_WIDTH), lambda bi, i: (bi, i, 0)),
                  pl.BlockSpec((None, 1) + mem_kv.shape[2:], lambda bi, i: (layer, bi, 0, 0)),
                  pl.BlockSpec((None,) + w_out.shape[1:], lambda bi, i: (layer, 0, 0)),
                  pl.BlockSpec((1, tm, d), lambda bi, i: (bi, i, 0))],
        out_specs=pl.BlockSpec((1, tm, d), lambda bi, i: (bi, i, 0)),
        out_shape=jax.ShapeDtypeStruct((b, t, d), F32),
        compiler_params=_cparams("parallel", "parallel"),
        name="mix_out",
    )(y, q_mem, mem_kv, w_out, x)


def _pack_attn_in(w_in):
    cuts = [Q_LORA, Q_LORA + KV_LORA, Q_LORA + KV_LORA + IDX_DIM, Q_LORA + KV_LORA + IDX_DIM + IDX_HEADS]
    c_q, c_kv, k_idx, w_idx, q_mem = jnp.split(w_in, cuts, axis=-1)
    pad = jnp.zeros(w_in.shape[:-1] + (ATTN_IN_PAD - w_in.shape[-1],), w_in.dtype)
    return jnp.concatenate([c_q, c_kv, q_mem, k_idx, w_idx, pad], axis=-1)


def _pad_lanes(v):
    return jnp.pad(v, (0, LANES - v.shape[0])).reshape(1, LANES)


def kernel(x, mem, mixer_norm, ffn_norm, final_norm, mem_norm, w_mem_kv, w_out, w_gate, w_up, w_down,
           w_in_attn, q_norm, kv_norm, w_uq, w_uk, w_uv, w_idx_uq, idx_k_norm, idx_k_bias,
           w_in_pool, w_pool_group, pool_scale):
    b, t, d = x.shape
    depth = w_out.shape[0]
    n_mem = mem.shape[1]
    row = lambda v: v.reshape(1, -1)
    assert t % DSA_TK == 0 and DSA_TK % DSA_TQ == 0 and t % MIX_TM == 0 and (b * t) % FFN_TM == 0
    assert w_gate.shape[2] % FFN_TF == 0 and (b * n_mem) % (FFN_TM // 2) == 0

    w_kv_b, w_out_b = w_mem_kv.astype(BF16), w_out.astype(BF16)
    ffn_w = [w[0].astype(BF16) for w in (w_gate, w_up, w_down)]
    w_in_attn_b = _pack_attn_in(w_in_attn).astype(BF16)
    w_uq_t = jnp.swapaxes(w_uq, 1, 2).astype(BF16)
    w_iq_t = jnp.swapaxes(w_idx_uq, 1, 2).astype(BF16)
    w_uk_b, w_uv_b = w_uk.astype(BF16), w_uv.astype(BF16)
    w_in_pool_b, w_group_b = w_in_pool.astype(BF16), w_pool_group.astype(BF16)

    mem_kv = rms_matmul_layers(mem.reshape(b * n_mem, d), row(mem_norm), w_kv_b, tm=FFN_TM // 2, out_dtype=BF16)
    mem_kv = mem_kv.reshape(depth, b, n_mem, 2 * MEM_WIDTH)

    for i in range(depth):
        j = i // 2
        if i % 2 == 0:
            ckv, ckv_t, kidx, q_mem, q_abs_t, q_idx_t, w_idx_t = attn_proj(
                x, row(mixer_norm[i]), w_in_attn_b, row(q_norm[j]), row(kv_norm[j]),
                _pad_lanes(idx_k_norm[j]), _pad_lanes(idx_k_bias[j]), w_uq_t, w_uk_b, w_iq_t, j,
                tm=max(MIX_TM, DSA_TQ), tq=DSA_TQ)
            y = dsa_attention(q_idx_t, w_idx_t, q_abs_t, kidx, ckv, ckv_t, w_uv_b, j, tq=DSA_TQ, tk=DSA_TK)
            x = mix_out(y, q_mem, mem_kv, w_out_b, i, x, tm=MIX_TM)
        else:
            y, q_mem = pool_layer(x, row(mixer_norm[i]), w_in_pool_b, w_group_b, row(pool_scale[j]), j, tm=MIX_TM)
            x = mix_out(y, q_mem, mem_kv, w_out_b, i, x, tm=MIX_TM)
        last = i == depth - 1
        x, ffn_w = ffn(x.reshape(b * t, d), row(ffn_norm[i]), *ffn_w, row(final_norm),
                       None if last else (w_gate, w_up, w_down, i + 1), tm=FFN_TM, tf=FFN_TF, final_norm=last)
        x = x.reshape(b, t, d)
    return x
```

```python
import functools

import jax
import jax.numpy as jnp
from jax import lax
from jax.experimental import pallas as pl
from jax.experimental.pallas import tpu as pltpu

BF16 = jnp.bfloat16
F32 = jnp.float32
I32 = jnp.int32
I16 = jnp.int16

EPS = 1e-6
HEAD_DIM = 128
N_ATTN_HEADS = 12
Q_LORA = 512
KV_LORA = 256
IDX_HEADS = 16
IDX_DIM = 64
TOPK_MAX = 256
POOL_WINDOWS = (2, 4, 8, 16)
POOL_GROUP = 384
POOL_WIDTH = 1536
MEM_WIDTH = 512
N_MEM_HEADS = 4
POOL_HALO = 32
SUBLANES = 8

LANES = 128
_CQ0, _CKV0, _QM0, _KI0 = 0, Q_LORA, Q_LORA + KV_LORA, Q_LORA + KV_LORA + MEM_WIDTH
ATTN_IN_PAD = _KI0 + LANES

LOG2E = 1.4426950408889634
INT_MIN = -(2 ** 31)
INT_MAX = 2 ** 31 - 1
I16_MIN = -(2 ** 15)
ONES_ROWS = 16
NEG = -1e30

VMEM_LIMIT = 60 * 1024 * 1024
MXU_COLS = 256
DSA_TQ = 256
DSA_TK = 512
FFN_TM, FFN_TF = 1024, 512
MIX_TM = 512


def _cparams(*sem):
    return pltpu.CompilerParams(dimension_semantics=sem, vmem_limit_bytes=VMEM_LIMIT)


def _multi_chain_sum(parts, chains=4):
    accs = list(parts[:chains])
    for n, part in enumerate(parts[chains:]):
        accs[n % chains] = accs[n % chains] + part
    while len(accs) > 1:
        accs = [a + b for a, b in zip(accs[0::2], accs[1::2])] + accs[len(accs) & ~1:]
    return accs[0]


def _loop_by_two(n, body, init):
    def two(p, carry):
        return body(2 * p + 1, body(2 * p, carry))
    carry = lax.fori_loop(0, n // 2, two, init)
    return lax.fori_loop((n // 2) * 2, n, body, carry)


def _rms(xf, g):
    ms = jnp.mean(xf * xf, axis=-1, keepdims=True)
    return xf * lax.rsqrt(ms + EPS) * g


def _rms_matmul_kernel(x_ref, g_ref, w_ref, o_ref):
    h = _rms(x_ref[...], g_ref[...]).astype(BF16)
    o_ref[...] = jnp.dot(h, w_ref[...], preferred_element_type=F32).astype(o_ref.dtype)


def rms_matmul_layers(x, g, w, *, tm, out_dtype):
    m, d = x.shape
    n_layers, _, n = w.shape
    return pl.pallas_call(
        _rms_matmul_kernel,
        grid=(m // tm, n_layers),
        in_specs=[pl.BlockSpec((tm, d), lambda i, l: (i, 0)),
                  pl.BlockSpec((1, d), lambda i, l: (0, 0)),
                  pl.BlockSpec((None, d, n), lambda i, l: (l, 0, 0))],
        out_specs=pl.BlockSpec((None, tm, n), lambda i, l: (l, i, 0)),
        out_shape=jax.ShapeDtypeStruct((n_layers, m, n), out_dtype),
        compiler_params=_cparams("parallel", "parallel"),
        name="rms_matmul_layers",
    )(x, g, w)


def _ffn_kernel(x_ref, g_ref, wg_ref, wu_ref, wd_ref, fg_ref, *rest, final_norm, cast_next):
    if cast_next:
        ng_ref, nu_ref, nd_ref, o_ref, cg_ref, cu_ref, cd_ref, h_scr = rest
        cg_ref[...] = ng_ref[...].astype(BF16)
        cu_ref[...] = nu_ref[...].astype(BF16)
        cd_ref[...] = nd_ref[...].astype(BF16)
    else:
        o_ref, h_scr = rest
    f = pl.program_id(1)

    @pl.when(f == 0)
    def _():
        x = x_ref[...]
        h_scr[...] = _rms(x, g_ref[...]).astype(BF16)
        o_ref[...] = x

    h = h_scr[...]
    gate = jnp.dot(h, wg_ref[...], preferred_element_type=F32)
    up = jnp.dot(h, wu_ref[...], preferred_element_type=F32)
    act = (gate * jax.nn.sigmoid(gate) * up).astype(BF16)
    o_ref[...] += jnp.dot(act, wd_ref[...], preferred_element_type=F32)

    if final_norm:
        @pl.when(f == pl.num_programs(1) - 1)
        def _():
            o_ref[...] = _rms(o_ref[...], fg_ref[...])


def ffn(x, g, w_gate, w_up, w_down, final_g, next_f32, *, tm, tf, final_norm):
    m, d = x.shape
    dff = w_gate.shape[1]
    nm, nf = m // tm, dff // tf
    in_specs = [pl.BlockSpec((tm, d), lambda i, f: (i, 0)),
                pl.BlockSpec((1, d), lambda i, f: (0, 0)),
                pl.BlockSpec((d, tf), lambda i, f: (0, f)),
                pl.BlockSpec((d, tf), lambda i, f: (0, f)),
                pl.BlockSpec((tf, d), lambda i, f: (f, 0)),
                pl.BlockSpec((1, d), lambda i, f: (0, 0))]
    out_specs = [pl.BlockSpec((tm, d), lambda i, f: (i, 0))]
    out_shape = [jax.ShapeDtypeStruct((m, d), F32)]
    args = [x, g, w_gate, w_up, w_down, final_g]
    if next_f32 is not None:
        *stacked, layer = next_f32
        assert d % nm == 0 and (d // nm) % (2 * SUBLANES) == 0
        ds = d // nm
        in_specs += [pl.BlockSpec((None, ds, tf), lambda i, f: (layer, i, f)),
                     pl.BlockSpec((None, ds, tf), lambda i, f: (layer, i, f)),
                     pl.BlockSpec((None, tf, ds), lambda i, f: (layer, f, i))]
        out_specs += [pl.BlockSpec((ds, tf), lambda i, f: (i, f)),
                      pl.BlockSpec((ds, tf), lambda i, f: (i, f)),
                      pl.BlockSpec((tf, ds), lambda i, f: (f, i))]
        out_shape += [jax.ShapeDtypeStruct(w.shape[1:], BF16) for w in stacked]
        args += stacked
    out, *casted = pl.pallas_call(
        functools.partial(_ffn_kernel, final_norm=final_norm, cast_next=next_f32 is not None),
        grid=(nm, nf),
        in_specs=in_specs,
        out_specs=out_specs,
        out_shape=out_shape,
        scratch_shapes=[pltpu.VMEM((tm, d), BF16)],
        compiler_params=_cparams("parallel", "arbitrary"),
        name="ffn",
    )(*args)
    return out, casted


def _attn_proj_kernel(x_ref, g_ref, win_ref, qn_ref, kvn_ref, ig_ref, ib_ref, wuqt_ref, wuk_ref, wiqt_ref,
                      ckv_ref, ckvt_ref, kidx_ref, qmem_ref, qabst_ref, qidxt_ref, widxt_ref, *, tq):
    tm = x_ref.shape[1]
    h = _rms(x_ref[0], g_ref[...]).astype(BF16)
    proj = jnp.dot(h, win_ref[...], preferred_element_type=F32)
    c_q_t = _rms(proj[:, _CQ0:_CQ0 + Q_LORA], qn_ref[...]).T.astype(BF16)
    c_kv = _rms(proj[:, _CKV0:_CKV0 + KV_LORA], kvn_ref[...])
    ckv_ref[0] = c_kv.astype(BF16)
    ckvt_ref[0, 0:KV_LORA, :] = c_kv.T.astype(BF16)
    ckvt_ref[0, KV_LORA:KV_LORA + ONES_ROWS, :] = jnp.ones((ONES_ROWS, tm), BF16)
    qmem_ref[0] = proj[:, _QM0:_QM0 + MEM_WIDTH].astype(BF16)
    slab = proj[:, _KI0:_KI0 + LANES]
    live = lax.broadcasted_iota(I32, slab.shape, 1) < IDX_DIM
    mu = jnp.sum(jnp.where(live, slab, 0.0), axis=-1, keepdims=True) * (1.0 / IDX_DIM)
    cen = jnp.where(live, slab - mu, 0.0)
    var = jnp.sum(cen * cen, axis=-1, keepdims=True) * (1.0 / IDX_DIM)
    kin = cen * lax.rsqrt(var + EPS) * ig_ref[...] + ib_ref[...]
    kidx_ref[0] = kin[:, :IDX_DIM].astype(BF16)
    widxt_ref[0] = slab.T[IDX_DIM:IDX_DIM + IDX_HEADS, :] * (IDX_HEADS ** -0.5 * IDX_DIM ** -0.5)
    q_t = jnp.dot(wuqt_ref[...], c_q_t, preferred_element_type=F32).astype(BF16)
    for hd in range(N_ATTN_HEADS):
        qa_t = jnp.dot(wuk_ref[hd], q_t[hd * HEAD_DIM:(hd + 1) * HEAD_DIM, :], preferred_element_type=F32)
        qa_t = (qa_t * (HEAD_DIM ** -0.5 * LOG2E)).astype(BF16)
        for j in range(tm // tq):
            qabst_ref[0, j, :, hd * tq:(hd + 1) * tq] = qa_t[:, j * tq:(j + 1) * tq]
    qi_t = jnp.dot(wiqt_ref[...], c_q_t, preferred_element_type=F32).astype(BF16)
    for hd in range(IDX_HEADS):
        for j in range(tm // tq):
            qidxt_ref[0, j, :, hd * tq:(hd + 1) * tq] = qi_t[hd * IDX_DIM:(hd + 1) * IDX_DIM, j * tq:(j + 1) * tq]


def attn_proj(x, g, w_in, q_norm, kv_norm, idx_g, idx_b, w_uq_t, w_uk, w_iq_t, layer, *, tm, tq):
    b, t, d = x.shape
    full = lambda shape: pl.BlockSpec(shape, lambda bi, i: (0,) * len(shape))
    stacked = lambda w: pl.BlockSpec((None,) + w.shape[1:], lambda bi, i: (layer,) + (0,) * (w.ndim - 1))
    per_tile = tm // tq
    return pl.pallas_call(
        functools.partial(_attn_proj_kernel, tq=tq),
        grid=(b, t // tm),
        in_specs=[pl.BlockSpec((1, tm, d), lambda bi, i: (bi, i, 0)),
                  full((1, d)), stacked(w_in), full((1, Q_LORA)), full((1, KV_LORA)),
                  full((1, LANES)), full((1, LANES)), stacked(w_uq_t), stacked(w_uk), stacked(w_iq_t)],
        out_specs=[pl.BlockSpec((1, tm, KV_LORA), lambda bi, i: (bi, i, 0)),
                   pl.BlockSpec((1, KV_LORA + ONES_ROWS, tm), lambda bi, i: (bi, 0, i)),
                   pl.BlockSpec((1, tm, IDX_DIM), lambda bi, i: (bi, i, 0)),
                   pl.BlockSpec((1, tm, MEM_WIDTH), lambda bi, i: (bi, i, 0)),
                   pl.BlockSpec((1, per_tile, KV_LORA, N_ATTN_HEADS * tq), lambda bi, i: (bi, i, 0, 0)),
                   pl.BlockSpec((1, per_tile, IDX_DIM, IDX_HEADS * tq), lambda bi, i: (bi, i, 0, 0)),
                   pl.BlockSpec((1, IDX_HEADS, tm), lambda bi, i: (bi, 0, i))],
        out_shape=[jax.ShapeDtypeStruct((b, t, KV_LORA), BF16),
                   jax.ShapeDtypeStruct((b, KV_LORA + ONES_ROWS, t), BF16),
                   jax.ShapeDtypeStruct((b, t, IDX_DIM), BF16),
                   jax.ShapeDtypeStruct((b, t, MEM_WIDTH), BF16),
                   jax.ShapeDtypeStruct((b, t // tq, KV_LORA, N_ATTN_HEADS * tq), BF16),
                   jax.ShapeDtypeStruct((b, t // tq, IDX_DIM, IDX_HEADS * tq), BF16),
                   jax.ShapeDtypeStruct((b, IDX_HEADS, t), F32)],
        compiler_params=_cparams("parallel", "parallel"),
        name="attn_proj",
    )(x, g, w_in, q_norm, kv_norm, idx_g, idx_b, w_uq_t, w_uk, w_iq_t)


def _dsa_kernel(qidxt_ref, widxt_ref, qabst_ref, kidx_ref, ckv_ref, ckvt_ref, wuv_ref, y_ref,
                key_scr, half_scr, m_scr, acc_scr, *, tq, tk, k_top):
    i = pl.program_id(1)
    q0 = i * tq
    n_chunks = (q0 + tq + tk - 1) // tk
    qpos = q0 + lax.broadcasted_iota(I32, (1, tq), 1)
    hpt = max(1, MXU_COLS // tq)

    w = widxt_ref[0]

    def score_body(c, carry):
        k0 = pl.multiple_of(c * tk, tk)
        kc = kidx_ref[0, pl.ds(k0, tk), :]
        score = None
        for j in range(IDX_HEADS // hpt):
            s = jnp.dot(kc, qidxt_ref[0, 0, :, j * hpt * tq:(j + 1) * hpt * tq], preferred_element_type=F32)
            for u in range(hpt):
                hd = hpt * j + u
                term = jnp.maximum(s[:, u * tq:(u + 1) * tq], 0.0) * w[hd:hd + 1, :]
                score = term if score is None else score + term
        score = jnp.where(score == 0.0, 0.0, score)
        bits = pltpu.bitcast(score, I32)
        key = bits ^ ((bits >> 31) & 0x7FFFFFFF)
        causal = k0 + lax.broadcasted_iota(I32, (tk, 1), 0) <= qpos
        key_scr[pl.ds(k0, tk), :] = jnp.where(causal, key, INT_MIN)
        half_scr[pl.ds(k0, tk), :] = (jnp.where(causal, key, INT_MIN) >> 16).astype(I16)
        return carry

    _loop_by_two(n_chunks, score_body, 0)

    def count_rows(hit_fn):
        def body(c, acc):
            k0 = pl.multiple_of(c * tk, tk)
            kpos = k0 + lax.broadcasted_iota(I32, (tk, 1), 0)
            hit = jnp.where(hit_fn(key_scr[pl.ds(k0, tk), :], kpos), 1, 0).reshape(tk // 8, 8, tq)
            return acc + _multi_chain_sum([hit[g] for g in range(tk // 8)])
        acc = lax.fori_loop(0, n_chunks, body, jnp.zeros((8, tq), I32))
        return jnp.sum(acc, axis=0, keepdims=True)

    def count_ge(cand):
        return count_rows(lambda kk, kpos: kk >= cand)

    def count_half(hit_fn):
        one, zero = jnp.ones((), I16), jnp.zeros((), I16)

        def body(c, acc):
            k0 = pl.multiple_of(c * tk, tk)
            hit = jnp.where(hit_fn(half_scr[pl.ds(k0, tk), :]), one, zero).reshape(tk // 16, 16, tq)
            part = _multi_chain_sum([hit[g] for g in range(tk // 16)])
            return acc + part.astype(I32)
        acc = _loop_by_two(n_chunks, body, jnp.zeros((16, tq), I32))
        return jnp.sum(acc, axis=0, keepdims=True)

    def half_search(base):
        def body(b, state):
            v, cnt = state
            cand = v + lax.shift_left(jnp.int32(1), 15 - b)
            c = base + count_half(lambda hh: hh >= cand.astype(I16))
            take = c >= k_top
            return jnp.where(take, cand, v), jnp.where(take, c, cnt)
        return lax.fori_loop(0, 16, body, (jnp.full((1, tq), I16_MIN, I32), jnp.full((1, tq), -1, I32)))

    top, cnt_top = half_search(jnp.zeros((1, tq), I32))
    above = count_half(lambda hh: hh > top.astype(I16))

    def low_body(c, carry):
        k0 = pl.multiple_of(c * tk, tk)
        kk = key_scr[pl.ds(k0, tk), :]
        low = jnp.where((kk >> 16) == top, (kk & 0xFFFF) + I16_MIN, I16_MIN)
        half_scr[pl.ds(k0, tk), :] = low.astype(I16)
        return carry

    lax.fori_loop(0, n_chunks, low_body, 0)
    low, cnt_low = half_search(above)
    thr = top * 65536 + (low - I16_MIN)
    cnt = jnp.where(cnt_low >= 0, cnt_low, cnt_top)

    @pl.when(jnp.sum(jnp.where(cnt > k_top, 1, 0)) > 0)
    def _():
        at_max = thr == INT_MAX
        above = jnp.where(at_max, 0, count_ge(jnp.where(at_max, thr, thr + 1)))
        need = k_top - above
        pos_bits = max(1, (key_scr.shape[0] - 1).bit_length())

        def pos_body(b, lo):
            cand = lo + lax.shift_left(jnp.int32(1), pos_bits - 1 - b)
            before = count_rows(lambda kk, kpos: (kk == thr) & (kpos < cand))
            return jnp.where(before < need, cand, lo)

        last = lax.fori_loop(0, pos_bits, pos_body, jnp.zeros((1, tq), I32))

        def drop_body(c, carry):
            k0 = pl.multiple_of(c * tk, tk)
            kpos = k0 + lax.broadcasted_iota(I32, (tk, 1), 0)
            kk = key_scr[pl.ds(k0, tk), :]
            key_scr[pl.ds(k0, tk), :] = jnp.where((kk == thr) & (kpos > last), INT_MIN, kk)
            return carry

        lax.fori_loop(0, n_chunks, drop_body, 0)

    thr = jnp.maximum(thr, INT_MIN + 1)

    m_scr[...] = jnp.full(m_scr.shape, NEG, F32)
    acc_scr[...] = jnp.zeros(acc_scr.shape, F32)

    n_tiles = N_ATTN_HEADS // hpt
    tile_w = hpt * tq

    def softmax_tile(j, kv, bias):
        lt = jnp.dot(kv, qabst_ref[0, 0, :, j * tile_w:(j + 1) * tile_w], preferred_element_type=F32)
        ps, alphas = [], []
        for u in range(hpt):
            hc = slice((hpt * j + u) * tq, (hpt * j + u + 1) * tq)
            lg = lt[:, u * tq:(u + 1) * tq] + bias
            m_prev = m_scr[:, hc]
            m_new = jnp.maximum(m_prev, jnp.max(lg, axis=0, keepdims=True))
            ps.append(jnp.exp2(lg - m_new).astype(BF16))
            alphas.append(jnp.exp2(m_prev - m_new))
            m_scr[:, hc] = m_new
        return jnp.concatenate(ps, axis=1), jnp.concatenate(alphas, axis=1)

    def accumulate(j, kv_t, p, alpha):
        cols = slice(j * tile_w, (j + 1) * tile_w)
        acc_scr[:, cols] = alpha * acc_scr[:, cols] + jnp.dot(kv_t, p, preferred_element_type=F32)

    def att_body(c, carry):
        k0 = pl.multiple_of(c * tk, tk)
        kv = ckv_ref[0, pl.ds(k0, tk), :]
        kv_t = ckvt_ref[0, :, pl.ds(k0, tk)]
        bias = jnp.where(key_scr[pl.ds(k0, tk), :] >= thr, 0.0, NEG)
        for j in range(n_tiles):
            accumulate(j, kv_t, *softmax_tile(j, kv, bias))
        return carry

    lax.fori_loop(0, n_chunks, att_body, 0)

    for hd in range(N_ATTN_HEADS):
        hc = slice(hd * tq, (hd + 1) * tq)
        den = acc_scr[KV_LORA:KV_LORA + 1, hc]
        o_lat = (acc_scr[0:KV_LORA, hc] * (1.0 / den)).T.astype(BF16)
        y_ref[0, :, hd * HEAD_DIM:(hd + 1) * HEAD_DIM] = jnp.dot(
            o_lat, wuv_ref[hd], preferred_element_type=F32).astype(y_ref.dtype)


def dsa_attention(q_idx_t, w_idx_t, q_abs_t, k_idx, c_kv, c_kv_t, w_uv, layer, *, tq, tk):
    b, t, _ = c_kv.shape
    k_top = min(TOPK_MAX, t // 4)
    return pl.pallas_call(
        functools.partial(_dsa_kernel, tq=tq, tk=tk, k_top=k_top),
        grid=(b, t // tq),
        in_specs=[pl.BlockSpec((1, 1, IDX_DIM, IDX_HEADS * tq), lambda bi, i: (bi, i, 0, 0)),
                  pl.BlockSpec((1, IDX_HEADS, tq), lambda bi, i: (bi, 0, i)),
                  pl.BlockSpec((1, 1, KV_LORA, N_ATTN_HEADS * tq), lambda bi, i: (bi, i, 0, 0)),
                  pl.BlockSpec((1, t, IDX_DIM), lambda bi, i: (bi, 0, 0)),
                  pl.BlockSpec((1, t, KV_LORA), lambda bi, i: (bi, 0, 0)),
                  pl.BlockSpec((1, KV_LORA + ONES_ROWS, t), lambda bi, i: (bi, 0, 0)),
                  pl.BlockSpec((None,) + w_uv.shape[1:], lambda bi, i: (layer, 0, 0, 0))],
        out_specs=pl.BlockSpec((1, tq, N_ATTN_HEADS * HEAD_DIM), lambda bi, i: (bi, i, 0)),
        out_shape=jax.ShapeDtypeStruct((b, t, N_ATTN_HEADS * HEAD_DIM), BF16),
        scratch_shapes=[pltpu.VMEM((t, tq), I32),
                        pltpu.VMEM((t, tq), I16),
                        pltpu.VMEM((1, N_ATTN_HEADS * tq), F32),
                        pltpu.VMEM((KV_LORA + ONES_ROWS, N_ATTN_HEADS * tq), F32)],
        compiler_params=_cparams("parallel", "parallel"),
        name="dsa_attention",
    )(q_idx_t, w_idx_t, q_abs_t, k_idx, c_kv, c_kv_t, w_uv)


def _pool_kernel(x_ref, xh_ref, g_ref, win_ref, wg_ref, sc_ref, y_ref, qm_ref, ext_scr, a_scr, b_scr, *, tm):
    i = pl.program_id(1)
    rows = jnp.concatenate([xh_ref[0], x_ref[0]], axis=0)
    proj = jnp.dot(_rms(rows, g_ref[...]).astype(BF16), win_ref[...], preferred_element_type=F32)
    qm_ref[0] = proj[POOL_HALO:, POOL_WIDTH:].astype(qm_ref.dtype)
    ext_scr[...] = proj[:, :POOL_WIDTH]

    @pl.when(i == 0)
    def _():
        ext_scr[0:POOL_HALO, :] = jnp.zeros((POOL_HALO, POOL_WIDTH), F32)

    n_rows = POOL_HALO + tm
    pos = i * tm + lax.broadcasted_iota(I32, (tm, 1), 0)
    for g, win in enumerate(POOL_WINDOWS):
        lo = g * POOL_GROUP
        tok = ext_scr[POOL_HALO:POOL_HALO + tm, lo:lo + POOL_GROUP]
        src, width, start = ext_scr.at[:, lo:lo + POOL_GROUP], 1, 0
        for dst in (a_scr, b_scr, a_scr)[:max(0, win.bit_length() - 2)]:
            start += SUBLANES
            dst[start:, :] = src[start:, :] + src[start - width:n_rows - width, :]
            src, width = dst, 2 * width
        total = src[POOL_HALO:, :] + src[POOL_HALO - width:n_rows - width, :]
        cnt = jnp.minimum(pos + 1, win).astype(F32)
        mixed = (total / cnt - tok).astype(BF16)
        out = jnp.dot(mixed, wg_ref[g], preferred_element_type=F32)
        y_ref[0, :, lo:lo + POOL_GROUP] = (out * sc_ref[:, lo:lo + POOL_GROUP]).astype(y_ref.dtype)


def pool_layer(x, g, w_in, w_group, scale, layer, *, tm):
    b, t, d = x.shape
    halo_blocks = tm // POOL_HALO
    return pl.pallas_call(
        functools.partial(_pool_kernel, tm=tm),
        grid=(b, t // tm),
        in_specs=[pl.BlockSpec((1, tm, d), lambda bi, i: (bi, i, 0)),
                  pl.BlockSpec((1, POOL_HALO, d), lambda bi, i: (bi, jnp.maximum(i * halo_blocks - 1, 0), 0)),
                  pl.BlockSpec((1, d), lambda bi, i: (0, 0)),
                  pl.BlockSpec((None,) + w_in.shape[1:], lambda bi, i: (layer, 0, 0)),
                  pl.BlockSpec((None,) + w_group.shape[1:], lambda bi, i: (layer, 0, 0, 0)),
                  pl.BlockSpec((1, POOL_WIDTH), lambda bi, i: (0, 0))],
        out_specs=[pl.BlockSpec((1, tm, POOL_WIDTH), lambda bi, i: (bi, i, 0)),
                   pl.BlockSpec((1, tm, MEM_WIDTH), lambda bi, i: (bi, i, 0))],
        out_shape=[jax.ShapeDtypeStruct((b, t, POOL_WIDTH), BF16),
                   jax.ShapeDtypeStruct((b, t, MEM_WIDTH), BF16)],
        scratch_shapes=[pltpu.VMEM((POOL_HALO + tm, POOL_WIDTH), F32),
                        pltpu.VMEM((POOL_HALO + tm, POOL_GROUP), F32),
                        pltpu.VMEM((POOL_HALO + tm, POOL_GROUP), F32)],
        compiler_params=_cparams("parallel", "parallel"),
        name="pool_layer",
    )(x, x, g, w_in, w_group, scale)


def _mix_out_kernel(y_ref, qm_ref, mkv_ref, wo_ref, x_ref, *rest, n_cast):
    o_ref = rest[n_cast]
    for src_ref, dst_ref in zip(rest[:n_cast], rest[n_cast + 1:]):
        dst_ref[...] = src_ref[...].astype(BF16)
    qm = qm_ref[0]
    mkv = mkv_ref[0]
    parts = [y_ref[0]]
    for hd in range(N_MEM_HEADS):
        lo = hd * HEAD_DIM
        logits = lax.dot_general(qm[:, lo:lo + HEAD_DIM], mkv[:, lo:lo + HEAD_DIM],
                                 (((1,), (1,)), ((), ())), preferred_element_type=F32) * HEAD_DIM ** -0.5
        p = jnp.exp(logits - jnp.max(logits, axis=-1, keepdims=True))
        den = jnp.sum(p, axis=-1, keepdims=True)
        att = jnp.dot(p.astype(BF16), mkv[:, MEM_WIDTH + lo:MEM_WIDTH + lo + HEAD_DIM],
                      preferred_element_type=F32)
        parts.append((att / den).astype(BF16))
    mix = jnp.concatenate(parts, axis=-1)
    o_ref[0] = x_ref[0] + jnp.dot(mix, wo_ref[...], preferred_element_type=F32)


def mix_out(y, q_mem, mem_kv, w_out, layer, x, cast_f32=(), cast_layer=0, *, tm):
    b, t, d = x.shape
    nt = t // tm
    in_specs = [pl.BlockSpec((1, tm, y.shape[-1]), lambda bi, i: (bi, i, 0)),
                pl.BlockSpec((1, tm, MEM_WIDTH), lambda bi, i: (bi, i, 0)),
                pl.BlockSpec((None, 1) + mem_kv.shape[2:], lambda bi, i: (layer, bi, 0, 0)),
                pl.BlockSpec((None,) + w_out.shape[1:], lambda bi, i: (layer, 0, 0)),
                pl.BlockSpec((1, tm, d), lambda bi, i: (bi, i, 0))]
    out_specs = [pl.BlockSpec((1, tm, d), lambda bi, i: (bi, i, 0))]
    out_shape = [jax.ShapeDtypeStruct((b, t, d), F32)]
    for w in cast_f32:
        _, rows, cols = w.shape
        assert rows % (b * nt) == 0 and (rows // (b * nt)) % (2 * SUBLANES) == 0
        slab = rows // (b * nt)
        in_specs.append(pl.BlockSpec((None, slab, cols), lambda bi, i: (cast_layer, bi * nt + i, 0)))
        out_specs.append(pl.BlockSpec((slab, cols), lambda bi, i: (bi * nt + i, 0)))
        out_shape.append(jax.ShapeDtypeStruct((rows, cols), BF16))
    out, *casted = pl.pallas_call(
        functools.partial(_mix_out_kernel, n_cast=len(cast_f32)),
        grid=(b, nt),
        in_specs=in_specs,
        out_specs=out_specs,
        out_shape=out_shape,
        compiler_params=_cparams("parallel", "parallel"),
        name="mix_out",
    )(y, q_mem, mem_kv, w_out, x, *cast_f32)
    return out, casted


def _pack_attn_in(w_in):
    cuts = [Q_LORA, Q_LORA + KV_LORA, Q_LORA + KV_LORA + IDX_DIM, Q_LORA + KV_LORA + IDX_DIM + IDX_HEADS]
    c_q, c_kv, k_idx, w_idx, q_mem = jnp.split(w_in, cuts, axis=-1)
    pad = jnp.zeros(w_in.shape[:-1] + (ATTN_IN_PAD - w_in.shape[-1],), w_in.dtype)
    return jnp.concatenate([c_q, c_kv, q_mem, k_idx, w_idx, pad], axis=-1)


def _pad_lanes(v):
    return jnp.pad(v, (0, LANES - v.shape[0])).reshape(1, LANES)


def kernel(x, mem, mixer_norm, ffn_norm, final_norm, mem_norm, w_mem_kv, w_out, w_gate, w_up, w_down,
           w_in_attn, q_norm, kv_norm, w_uq, w_uk, w_uv, w_idx_uq, idx_k_norm, idx_k_bias,
           w_in_pool, w_pool_group, pool_scale):
    b, t, d = x.shape
    depth = w_out.shape[0]
    n_mem = mem.shape[1]
    row = lambda v: v.reshape(1, -1)
    assert t % DSA_TK == 0 and DSA_TK % DSA_TQ == 0 and t % MIX_TM == 0 and (b * t) % FFN_TM == 0
    assert w_gate.shape[2] % FFN_TF == 0 and (b * n_mem) % (FFN_TM // 2) == 0

    w_kv_b, w_out_b = w_mem_kv.astype(BF16), w_out.astype(BF16)
    ffn_f32 = (w_gate, w_up, w_down)
    w_in_attn_b = _pack_attn_in(w_in_attn).astype(BF16)
    w_uq_t = jnp.swapaxes(w_uq, 1, 2).astype(BF16)
    w_iq_t = jnp.swapaxes(w_idx_uq, 1, 2).astype(BF16)
    w_uk_b, w_uv_b = w_uk.astype(BF16), w_uv.astype(BF16)
    w_in_pool_b, w_group_b = w_in_pool.astype(BF16), w_pool_group.astype(BF16)

    mem_kv = rms_matmul_layers(mem.reshape(b * n_mem, d), row(mem_norm), w_kv_b, tm=FFN_TM // 2, out_dtype=BF16)
    mem_kv = mem_kv.reshape(depth, b, n_mem, 2 * MEM_WIDTH)

    for i in range(depth):
        j = i // 2
        if i % 2 == 0:
            ckv, ckv_t, kidx, q_mem, q_abs_t, q_idx_t, w_idx_t = attn_proj(
                x, row(mixer_norm[i]), w_in_attn_b, row(q_norm[j]), row(kv_norm[j]),
                _pad_lanes(idx_k_norm[j]), _pad_lanes(idx_k_bias[j]), w_uq_t, w_uk_b, w_iq_t, j,
                tm=max(MIX_TM, DSA_TQ), tq=DSA_TQ)
            y = dsa_attention(q_idx_t, w_idx_t, q_abs_t, kidx, ckv, ckv_t, w_uv_b, j, tq=DSA_TQ, tk=DSA_TK)
        else:
            y, q_mem = pool_layer(x, row(mixer_norm[i]), w_in_pool_b, w_group_b, row(pool_scale[j]), j, tm=MIX_TM)
        x, casted = mix_out(y, q_mem, mem_kv, w_out_b, i, x, ffn_f32 if i == 0 else (), tm=MIX_TM)
        if casted:
            ffn_w = casted
        last = i == depth - 1
        x, ffn_w = ffn(x.reshape(b * t, d), row(ffn_norm[i]), *ffn_w, row(final_norm),
                       None if last else (*ffn_f32, i + 1), tm=FFN_TM, tf=FFN_TF, final_norm=last)
        x = x.reshape(b, t, d)
    return x
```

```python
import functools

import jax
import jax.numpy as jnp
from jax import lax
from jax.experimental import pallas as pl
from jax.experimental.pallas import tpu as pltpu

BF16 = jnp.bfloat16
F32 = jnp.float32
I32 = jnp.int32
I16 = jnp.int16

EPS = 1e-6
HEAD_DIM = 128
N_ATTN_HEADS = 12
Q_LORA = 512
KV_LORA = 256
IDX_HEADS = 16
IDX_DIM = 64
TOPK_MAX = 256
POOL_WINDOWS = (2, 4, 8, 16)
POOL_GROUP = 384
POOL_WIDTH = 1536
MEM_WIDTH = 512
N_MEM_HEADS = 4
POOL_HALO = 32
SUBLANES = 8

LANES = 128
_CQ0, _CKV0, _QM0, _KI0 = 0, Q_LORA, Q_LORA + KV_LORA, Q_LORA + KV_LORA + MEM_WIDTH
ATTN_IN_PAD = _KI0 + LANES

LOG2E = 1.4426950408889634
INT_MIN = -(2 ** 31)
INT_MAX = 2 ** 31 - 1
I16_MIN = -(2 ** 15)
ONES_ROWS = 16
NEG = -1e30

VMEM_LIMIT = 60 * 1024 * 1024
MXU_COLS = 256
DSA_TQ = 256
DSA_TK = 512
FFN_TM, FFN_TF = 1024, 512
MIX_TM = 512


def _cparams(*sem):
    return pltpu.CompilerParams(dimension_semantics=sem, vmem_limit_bytes=VMEM_LIMIT)


def _multi_chain_sum(parts, chains=4):
    accs = list(parts[:chains])
    for n, part in enumerate(parts[chains:]):
        accs[n % chains] = accs[n % chains] + part
    while len(accs) > 1:
        accs = [a + b for a, b in zip(accs[0::2], accs[1::2])] + accs[len(accs) & ~1:]
    return accs[0]


def _loop_by_two(n, body, init):
    def two(p, carry):
        return body(2 * p + 1, body(2 * p, carry))
    carry = lax.fori_loop(0, n // 2, two, init)
    return lax.fori_loop((n // 2) * 2, n, body, carry)


def _rms(xf, g):
    ms = jnp.mean(xf * xf, axis=-1, keepdims=True)
    return xf * lax.rsqrt(ms + EPS) * g


def _rms_matmul_kernel(x_ref, g_ref, w_ref, o_ref):
    h = _rms(x_ref[...], g_ref[...]).astype(BF16)
    o_ref[...] = jnp.dot(h, w_ref[...], preferred_element_type=F32).astype(o_ref.dtype)


def rms_matmul_layers(x, g, w, *, tm, out_dtype):
    m, d = x.shape
    n_layers, _, n = w.shape
    return pl.pallas_call(
        _rms_matmul_kernel,
        grid=(m // tm, n_layers),
        in_specs=[pl.BlockSpec((tm, d), lambda i, l: (i, 0)),
                  pl.BlockSpec((1, d), lambda i, l: (0, 0)),
                  pl.BlockSpec((None, d, n), lambda i, l: (l, 0, 0))],
        out_specs=pl.BlockSpec((None, tm, n), lambda i, l: (l, i, 0)),
        out_shape=jax.ShapeDtypeStruct((n_layers, m, n), out_dtype),
        compiler_params=_cparams("parallel", "parallel"),
        name="rms_matmul_layers",
    )(x, g, w)


def _ffn_kernel(x_ref, g_ref, wg_ref, wu_ref, wd_ref, fg_ref, *rest, final_norm, cast_next):
    if cast_next:
        ng_ref, nu_ref, nd_ref, o_ref, cg_ref, cu_ref, cd_ref, h_scr = rest
        cg_ref[...] = ng_ref[...].astype(BF16)
        cu_ref[...] = nu_ref[...].astype(BF16)
        cd_ref[...] = nd_ref[...].astype(BF16)
    else:
        o_ref, h_scr = rest
    f = pl.program_id(1)

    @pl.when(f == 0)
    def _():
        x = x_ref[...]
        h_scr[...] = _rms(x, g_ref[...]).astype(BF16)
        o_ref[...] = x

    h = h_scr[...]
    gate = jnp.dot(h, wg_ref[...], preferred_element_type=F32)
    up = jnp.dot(h, wu_ref[...], preferred_element_type=F32)
    act = (gate * jax.nn.sigmoid(gate) * up).astype(BF16)
    o_ref[...] += jnp.dot(act, wd_ref[...], preferred_element_type=F32)

    if final_norm:
        @pl.when(f == pl.num_programs(1) - 1)
        def _():
            o_ref[...] = _rms(o_ref[...], fg_ref[...])


def ffn(x, g, w_gate, w_up, w_down, final_g, next_f32, *, tm, tf, final_norm):
    m, d = x.shape
    dff = w_gate.shape[1]
    nm, nf = m // tm, dff // tf
    in_specs = [pl.BlockSpec((tm, d), lambda i, f: (i, 0)),
                pl.BlockSpec((1, d), lambda i, f: (0, 0)),
                pl.BlockSpec((d, tf), lambda i, f: (0, f)),
                pl.BlockSpec((d, tf), lambda i, f: (0, f)),
                pl.BlockSpec((tf, d), lambda i, f: (f, 0)),
                pl.BlockSpec((1, d), lambda i, f: (0, 0))]
    out_specs = [pl.BlockSpec((tm, d), lambda i, f: (i, 0))]
    out_shape = [jax.ShapeDtypeStruct((m, d), F32)]
    args = [x, g, w_gate, w_up, w_down, final_g]
    if next_f32 is not None:
        *stacked, layer = next_f32
        assert d % nm == 0 and (d // nm) % (2 * SUBLANES) == 0
        ds = d // nm
        in_specs += [pl.BlockSpec((None, ds, tf), lambda i, f: (layer, i, f)),
                     pl.BlockSpec((None, ds, tf), lambda i, f: (layer, i, f)),
                     pl.BlockSpec((None, tf, ds), lambda i, f: (layer, f, i))]
        out_specs += [pl.BlockSpec((ds, tf), lambda i, f: (i, f)),
                      pl.BlockSpec((ds, tf), lambda i, f: (i, f)),
                      pl.BlockSpec((tf, ds), lambda i, f: (f, i))]
        out_shape += [jax.ShapeDtypeStruct(w.shape[1:], BF16) for w in stacked]
        args += stacked
    out, *casted = pl.pallas_call(
        functools.partial(_ffn_kernel, final_norm=final_norm, cast_next=next_f32 is not None),
        grid=(nm, nf),
        in_specs=in_specs,
        out_specs=out_specs,
        out_shape=out_shape,
        scratch_shapes=[pltpu.VMEM((tm, d), BF16)],
        compiler_params=_cparams("parallel", "arbitrary"),
        name="ffn",
    )(*args)
    return out, casted


def _attn_proj_kernel(x_ref, g_ref, win_ref, qn_ref, kvn_ref, ig_ref, ib_ref, wuqt_ref, wuk_ref, wiqt_ref,
                      ckv_ref, ckvt_ref, kidx_ref, qmem_ref, qabst_ref, qidxt_ref, widxt_ref, *, tq):
    tm = x_ref.shape[1]
    h = _rms(x_ref[0], g_ref[...]).astype(BF16)
    proj = jnp.dot(h, win_ref[...], preferred_element_type=F32)
    c_q_t = _rms(proj[:, _CQ0:_CQ0 + Q_LORA], qn_ref[...]).T.astype(BF16)
    c_kv = _rms(proj[:, _CKV0:_CKV0 + KV_LORA], kvn_ref[...])
    ckv_ref[0] = c_kv.astype(BF16)
    ckvt_ref[0, 0:KV_LORA, :] = c_kv.T.astype(BF16)
    ckvt_ref[0, KV_LORA:KV_LORA + ONES_ROWS, :] = jnp.ones((ONES_ROWS, tm), BF16)
    qmem_ref[0] = proj[:, _QM0:_QM0 + MEM_WIDTH].astype(BF16)
    slab = proj[:, _KI0:_KI0 + LANES]
    live = lax.broadcasted_iota(I32, slab.shape, 1) < IDX_DIM
    mu = jnp.sum(jnp.where(live, slab, 0.0), axis=-1, keepdims=True) * (1.0 / IDX_DIM)
    cen = jnp.where(live, slab - mu, 0.0)
    var = jnp.sum(cen * cen, axis=-1, keepdims=True) * (1.0 / IDX_DIM)
    kin = cen * lax.rsqrt(var + EPS) * ig_ref[...] + ib_ref[...]
    kidx_ref[0] = kin[:, :IDX_DIM].astype(BF16)
    widxt_ref[0] = slab.T[IDX_DIM:IDX_DIM + IDX_HEADS, :] * (IDX_HEADS ** -0.5 * IDX_DIM ** -0.5)
    q_t = jnp.dot(wuqt_ref[...], c_q_t, preferred_element_type=F32).astype(BF16)
    for hd in range(N_ATTN_HEADS):
        qa_t = jnp.dot(wuk_ref[hd], q_t[hd * HEAD_DIM:(hd + 1) * HEAD_DIM, :], preferred_element_type=F32)
        qa_t = (qa_t * (HEAD_DIM ** -0.5 * LOG2E)).astype(BF16)
        for j in range(tm // tq):
            qabst_ref[0, j, :, hd * tq:(hd + 1) * tq] = qa_t[:, j * tq:(j + 1) * tq]
    qi_t = jnp.dot(wiqt_ref[...], c_q_t, preferred_element_type=F32).astype(BF16)
    for hd in range(IDX_HEADS):
        for j in range(tm // tq):
            qidxt_ref[0, j, :, hd * tq:(hd + 1) * tq] = qi_t[hd * IDX_DIM:(hd + 1) * IDX_DIM, j * tq:(j + 1) * tq]


def attn_proj(x, g, w_in, q_norm, kv_norm, idx_g, idx_b, w_uq_t, w_uk, w_iq_t, layer, *, tm, tq):
    b, t, d = x.shape
    full = lambda shape: pl.BlockSpec(shape, lambda bi, i: (0,) * len(shape))
    stacked = lambda w: pl.BlockSpec((None,) + w.shape[1:], lambda bi, i: (layer,) + (0,) * (w.ndim - 1))
    per_tile = tm // tq
    return pl.pallas_call(
        functools.partial(_attn_proj_kernel, tq=tq),
        grid=(b, t // tm),
        in_specs=[pl.BlockSpec((1, tm, d), lambda bi, i: (bi, i, 0)),
                  full((1, d)), stacked(w_in), full((1, Q_LORA)), full((1, KV_LORA)),
                  full((1, LANES)), full((1, LANES)), stacked(w_uq_t), stacked(w_uk), stacked(w_iq_t)],
        out_specs=[pl.BlockSpec((1, tm, KV_LORA), lambda bi, i: (bi, i, 0)),
                   pl.BlockSpec((1, KV_LORA + ONES_ROWS, tm), lambda bi, i: (bi, 0, i)),
                   pl.BlockSpec((1, tm, IDX_DIM), lambda bi, i: (bi, i, 0)),
                   pl.BlockSpec((1, tm, MEM_WIDTH), lambda bi, i: (bi, i, 0)),
                   pl.BlockSpec((1, per_tile, KV_LORA, N_ATTN_HEADS * tq), lambda bi, i: (bi, i, 0, 0)),
                   pl.BlockSpec((1, per_tile, IDX_DIM, IDX_HEADS * tq), lambda bi, i: (bi, i, 0, 0)),
                   pl.BlockSpec((1, IDX_HEADS, tm), lambda bi, i: (bi, 0, i))],
        out_shape=[jax.ShapeDtypeStruct((b, t, KV_LORA), BF16),
                   jax.ShapeDtypeStruct((b, KV_LORA + ONES_ROWS, t), BF16),
                   jax.ShapeDtypeStruct((b, t, IDX_DIM), BF16),
                   jax.ShapeDtypeStruct((b, t, MEM_WIDTH), BF16),
                   jax.ShapeDtypeStruct((b, t // tq, KV_LORA, N_ATTN_HEADS * tq), BF16),
                   jax.ShapeDtypeStruct((b, t // tq, IDX_DIM, IDX_HEADS * tq), BF16),
                   jax.ShapeDtypeStruct((b, IDX_HEADS, t), F32)],
        compiler_params=_cparams("parallel", "parallel"),
        name="attn_proj",
    )(x, g, w_in, q_norm, kv_norm, idx_g, idx_b, w_uq_t, w_uk, w_iq_t)


def _dsa_kernel(qidxt_ref, widxt_ref, qabst_ref, kidx_ref, ckv_ref, ckvt_ref, wuv_ref, y_ref,
                key_scr, half_scr, m_scr, acc_scr, *, tq, tk, k_top):
    assert tk in (tq, 2 * tq)
    i = pl.program_id(1)
    q0 = i * tq
    n_full = (q0 + tq) // tk
    has_tail = jnp.where((q0 + tq) % tk > 0, 1, 0)
    qpos = q0 + lax.broadcasted_iota(I32, (1, tq), 1)
    hpt = max(1, MXU_COLS // tq)

    def over_chunks(body, init, by_two=False):
        def full_body(c, carry):
            return body(pl.multiple_of(c * tk, tk), tk, carry)
        if by_two:
            carry = _loop_by_two(n_full, full_body, init)
        else:
            carry = lax.fori_loop(0, n_full, full_body, init)
        if tk == tq:
            return carry
        return lax.fori_loop(0, has_tail, lambda _, cy: body(pl.multiple_of(n_full * tk, tk), tq, cy), carry)

    w = widxt_ref[0]

    def score_body(k0, rows, carry):
        kc = kidx_ref[0, pl.ds(k0, rows), :]
        score = None
        for j in range(IDX_HEADS // hpt):
            s = jnp.dot(kc, qidxt_ref[0, 0, :, j * hpt * tq:(j + 1) * hpt * tq], preferred_element_type=F32)
            for u in range(hpt):
                hd = hpt * j + u
                term = jnp.maximum(s[:, u * tq:(u + 1) * tq], 0.0) * w[hd:hd + 1, :]
                score = term if score is None else score + term
        score = jnp.where(score == 0.0, 0.0, score)
        bits = pltpu.bitcast(score, I32)
        key = bits ^ ((bits >> 31) & 0x7FFFFFFF)
        causal = k0 + lax.broadcasted_iota(I32, (rows, 1), 0) <= qpos
        key_scr[pl.ds(k0, rows), :] = jnp.where(causal, key, INT_MIN)
        half_scr[pl.ds(k0, rows), :] = (jnp.where(causal, key, INT_MIN) >> 16).astype(I16)
        return carry

    over_chunks(score_body, 0, by_two=True)

    def count_rows(hit_fn):
        def body(k0, rows, acc):
            kpos = k0 + lax.broadcasted_iota(I32, (rows, 1), 0)
            hit = jnp.where(hit_fn(key_scr[pl.ds(k0, rows), :], kpos), 1, 0).reshape(rows // 8, 8, tq)
            return acc + _multi_chain_sum([hit[g] for g in range(rows // 8)])
        return jnp.sum(over_chunks(body, jnp.zeros((8, tq), I32)), axis=0, keepdims=True)

    def count_ge(cand):
        return count_rows(lambda kk, kpos: kk >= cand)

    def count_half(hit_fn):
        one, zero = jnp.ones((), I16), jnp.zeros((), I16)

        def body(k0, rows, acc):
            hit = jnp.where(hit_fn(half_scr[pl.ds(k0, rows), :]), one, zero).reshape(rows // 16, 16, tq)
            part = _multi_chain_sum([hit[g] for g in range(rows // 16)])
            return acc + part.astype(I32)
        return jnp.sum(over_chunks(body, jnp.zeros((16, tq), I32), by_two=True), axis=0, keepdims=True)

    def half_search(base):
        def body(b, state):
            v, cnt = state
            cand = v + lax.shift_left(jnp.int32(1), 15 - b)
            c = base + count_half(lambda hh: hh >= cand.astype(I16))
            take = c >= k_top
            return jnp.where(take, cand, v), jnp.where(take, c, cnt)
        return lax.fori_loop(0, 16, body, (jnp.full((1, tq), I16_MIN, I32), jnp.full((1, tq), -1, I32)))

    top, cnt_top = half_search(jnp.zeros((1, tq), I32))
    above = count_half(lambda hh: hh > top.astype(I16))

    def low_body(k0, rows, carry):
        kk = key_scr[pl.ds(k0, rows), :]
        low = jnp.where((kk >> 16) == top, (kk & 0xFFFF) + I16_MIN, I16_MIN)
        half_scr[pl.ds(k0, rows), :] = low.astype(I16)
        return carry

    over_chunks(low_body, 0)
    low, cnt_low = half_search(above)
    thr = top * 65536 + (low - I16_MIN)
    cnt = jnp.where(cnt_low >= 0, cnt_low, cnt_top)

    @pl.when(jnp.sum(jnp.where(cnt > k_top, 1, 0)) > 0)
    def _():
        at_max = thr == INT_MAX
        above = jnp.where(at_max, 0, count_ge(jnp.where(at_max, thr, thr + 1)))
        need = k_top - above
        pos_bits = max(1, (key_scr.shape[0] - 1).bit_length())

        def pos_body(b, lo):
            cand = lo + lax.shift_left(jnp.int32(1), pos_bits - 1 - b)
            before = count_rows(lambda kk, kpos: (kk == thr) & (kpos < cand))
            return jnp.where(before < need, cand, lo)

        last = lax.fori_loop(0, pos_bits, pos_body, jnp.zeros((1, tq), I32))

        def drop_body(k0, rows, carry):
            kpos = k0 + lax.broadcasted_iota(I32, (rows, 1), 0)
            kk = key_scr[pl.ds(k0, rows), :]
            key_scr[pl.ds(k0, rows), :] = jnp.where((kk == thr) & (kpos > last), INT_MIN, kk)
            return carry

        over_chunks(drop_body, 0)

    thr = jnp.maximum(thr, INT_MIN + 1)

    m_scr[...] = jnp.full(m_scr.shape, NEG, F32)
    acc_scr[...] = jnp.zeros(acc_scr.shape, F32)

    n_tiles = N_ATTN_HEADS // hpt
    tile_w = hpt * tq

    def softmax_tile(j, kv, bias):
        lt = jnp.dot(kv, qabst_ref[0, 0, :, j * tile_w:(j + 1) * tile_w], preferred_element_type=F32)
        ps, alphas = [], []
        for u in range(hpt):
            hc = slice((hpt * j + u) * tq, (hpt * j + u + 1) * tq)
            lg = lt[:, u * tq:(u + 1) * tq] + bias
            m_prev = m_scr[:, hc]
            m_new = jnp.maximum(m_prev, jnp.max(lg, axis=0, keepdims=True))
            ps.append(jnp.exp2(lg - m_new).astype(BF16))
            alphas.append(jnp.exp2(m_prev - m_new))
            m_scr[:, hc] = m_new
        return jnp.concatenate(ps, axis=1), jnp.concatenate(alphas, axis=1)

    def accumulate(j, kv_t, p, alpha):
        cols = slice(j * tile_w, (j + 1) * tile_w)
        acc_scr[:, cols] = alpha * acc_scr[:, cols] + jnp.dot(kv_t, p, preferred_element_type=F32)

    def att_body(k0, rows, carry):
        kv = ckv_ref[0, pl.ds(k0, rows), :]
        kv_t = ckvt_ref[0, :, pl.ds(k0, rows)]
        bias = jnp.where(key_scr[pl.ds(k0, rows), :] >= thr, 0.0, NEG)
        for j in range(n_tiles):
            accumulate(j, kv_t, *softmax_tile(j, kv, bias))
        return carry

    over_chunks(att_body, 0)

    for hd in range(N_ATTN_HEADS):
        hc = slice(hd * tq, (hd + 1) * tq)
        den = acc_scr[KV_LORA:KV_LORA + 1, hc]
        o_lat = (acc_scr[0:KV_LORA, hc] * (1.0 / den)).T.astype(BF16)
        y_ref[0, :, hd * HEAD_DIM:(hd + 1) * HEAD_DIM] = jnp.dot(
            o_lat, wuv_ref[hd], preferred_element_type=F32).astype(y_ref.dtype)


def dsa_attention(q_idx_t, w_idx_t, q_abs_t, k_idx, c_kv, c_kv_t, w_uv, layer, *, tq, tk):
    b, t, _ = c_kv.shape
    k_top = min(TOPK_MAX, t // 4)
    return pl.pallas_call(
        functools.partial(_dsa_kernel, tq=tq, tk=tk, k_top=k_top),
        grid=(b, t // tq),
        in_specs=[pl.BlockSpec((1, 1, IDX_DIM, IDX_HEADS * tq), lambda bi, i: (bi, i, 0, 0)),
                  pl.BlockSpec((1, IDX_HEADS, tq), lambda bi, i: (bi, 0, i)),
                  pl.BlockSpec((1, 1, KV_LORA, N_ATTN_HEADS * tq), lambda bi, i: (bi, i, 0, 0)),
                  pl.BlockSpec((1, t, IDX_DIM), lambda bi, i: (bi, 0, 0)),
                  pl.BlockSpec((1, t, KV_LORA), lambda bi, i: (bi, 0, 0)),
                  pl.BlockSpec((1, KV_LORA + ONES_ROWS, t), lambda bi, i: (bi, 0, 0)),
                  pl.BlockSpec((None,) + w_uv.shape[1:], lambda bi, i: (layer, 0, 0, 0))],
        out_specs=pl.BlockSpec((1, tq, N_ATTN_HEADS * HEAD_DIM), lambda bi, i: (bi, i, 0)),
        out_shape=jax.ShapeDtypeStruct((b, t, N_ATTN_HEADS * HEAD_DIM), BF16),
        scratch_shapes=[pltpu.VMEM((t, tq), I32),
                        pltpu.VMEM((t, tq), I16),
                        pltpu.VMEM((1, N_ATTN_HEADS * tq), F32),
                        pltpu.VMEM((KV_LORA + ONES_ROWS, N_ATTN_HEADS * tq), F32)],
        compiler_params=_cparams("parallel", "parallel"),
        name="dsa_attention",
    )(q_idx_t, w_idx_t, q_abs_t, k_idx, c_kv, c_kv_t, w_uv)


def _pool_kernel(x_ref, xh_ref, g_ref, win_ref, wg_ref, sc_ref, y_ref, qm_ref, ext_scr, a_scr, b_scr, *, tm):
    i = pl.program_id(1)
    rows = jnp.concatenate([xh_ref[0], x_ref[0]], axis=0)
    proj = jnp.dot(_rms(rows, g_ref[...]).astype(BF16), win_ref[...], preferred_element_type=F32)
    qm_ref[0] = proj[POOL_HALO:, POOL_WIDTH:].astype(qm_ref.dtype)
    ext_scr[...] = proj[:, :POOL_WIDTH]

    @pl.when(i == 0)
    def _():
        ext_scr[0:POOL_HALO, :] = jnp.zeros((POOL_HALO, POOL_WIDTH), F32)

    n_rows = POOL_HALO + tm
    pos = i * tm + lax.broadcasted_iota(I32, (tm, 1), 0)
    for g, win in enumerate(POOL_WINDOWS):
        lo = g * POOL_GROUP
        tok = ext_scr[POOL_HALO:POOL_HALO + tm, lo:lo + POOL_GROUP]
        src, width, start = ext_scr.at[:, lo:lo + POOL_GROUP], 1, 0
        for dst in (a_scr, b_scr, a_scr)[:max(0, win.bit_length() - 2)]:
            start += SUBLANES
            dst[start:, :] = src[start:, :] + src[start - width:n_rows - width, :]
            src, width = dst, 2 * width
        total = src[POOL_HALO:, :] + src[POOL_HALO - width:n_rows - width, :]
        cnt = jnp.minimum(pos + 1, win).astype(F32)
        mixed = (total / cnt - tok).astype(BF16)
        out = jnp.dot(mixed, wg_ref[g], preferred_element_type=F32)
        y_ref[0, :, lo:lo + POOL_GROUP] = (out * sc_ref[:, lo:lo + POOL_GROUP]).astype(y_ref.dtype)


def pool_layer(x, g, w_in, w_group, scale, layer, *, tm):
    b, t, d = x.shape
    halo_blocks = tm // POOL_HALO
    return pl.pallas_call(
        functools.partial(_pool_kernel, tm=tm),
        grid=(b, t // tm),
        in_specs=[pl.BlockSpec((1, tm, d), lambda bi, i: (bi, i, 0)),
                  pl.BlockSpec((1, POOL_HALO, d), lambda bi, i: (bi, jnp.maximum(i * halo_blocks - 1, 0), 0)),
                  pl.BlockSpec((1, d), lambda bi, i: (0, 0)),
                  pl.BlockSpec((None,) + w_in.shape[1:], lambda bi, i: (layer, 0, 0)),
                  pl.BlockSpec((None,) + w_group.shape[1:], lambda bi, i: (layer, 0, 0, 0)),
                  pl.BlockSpec((1, POOL_WIDTH), lambda bi, i: (0, 0))],
        out_specs=[pl.BlockSpec((1, tm, POOL_WIDTH), lambda bi, i: (bi, i, 0)),
                   pl.BlockSpec((1, tm, MEM_WIDTH), lambda bi, i: (bi, i, 0))],
        out_shape=[jax.ShapeDtypeStruct((b, t, POOL_WIDTH), BF16),
                   jax.ShapeDtypeStruct((b, t, MEM_WIDTH), BF16)],
        scratch_shapes=[pltpu.VMEM((POOL_HALO + tm, POOL_WIDTH), F32),
                        pltpu.VMEM((POOL_HALO + tm, POOL_GROUP), F32),
                        pltpu.VMEM((POOL_HALO + tm, POOL_GROUP), F32)],
        compiler_params=_cparams("parallel", "parallel"),
        name="pool_layer",
    )(x, x, g, w_in, w_group, scale)


def _mix_out_kernel(y_ref, qm_ref, mkv_ref, wo_ref, x_ref, *rest, n_cast):
    o_ref = rest[n_cast]
    for src_ref, dst_ref in zip(rest[:n_cast], rest[n_cast + 1:]):
        dst_ref[...] = src_ref[...].astype(BF16)
    qm = qm_ref[0]
    mkv = mkv_ref[0]
    parts = [y_ref[0]]
    for hd in range(N_MEM_HEADS):
        lo = hd * HEAD_DIM
        logits = lax.dot_general(qm[:, lo:lo + HEAD_DIM], mkv[:, lo:lo + HEAD_DIM],
                                 (((1,), (1,)), ((), ())), preferred_element_type=F32) * HEAD_DIM ** -0.5
        p = jnp.exp(logits - jnp.max(logits, axis=-1, keepdims=True))
        den = jnp.sum(p, axis=-1, keepdims=True)
        att = jnp.dot(p.astype(BF16), mkv[:, MEM_WIDTH + lo:MEM_WIDTH + lo + HEAD_DIM],
                      preferred_element_type=F32)
        parts.append((att / den).astype(BF16))
    mix = jnp.concatenate(parts, axis=-1)
    o_ref[0] = x_ref[0] + jnp.dot(mix, wo_ref[...], preferred_element_type=F32)


def mix_out(y, q_mem, mem_kv, w_out, layer, x, cast_f32=(), cast_layer=0, *, tm):
    b, t, d = x.shape
    nt = t // tm
    in_specs = [pl.BlockSpec((1, tm, y.shape[-1]), lambda bi, i: (bi, i, 0)),
                pl.BlockSpec((1, tm, MEM_WIDTH), lambda bi, i: (bi, i, 0)),
                pl.BlockSpec((None, 1) + mem_kv.shape[2:], lambda bi, i: (layer, bi, 0, 0)),
                pl.BlockSpec((None,) + w_out.shape[1:], lambda bi, i: (layer, 0, 0)),
                pl.BlockSpec((1, tm, d), lambda bi, i: (bi, i, 0))]
    out_specs = [pl.BlockSpec((1, tm, d), lambda bi, i: (bi, i, 0))]
    out_shape = [jax.ShapeDtypeStruct((b, t, d), F32)]
    for w in cast_f32:
        _, rows, cols = w.shape
        assert rows % (b * nt) == 0 and (rows // (b * nt)) % (2 * SUBLANES) == 0
        slab = rows // (b * nt)
        in_specs.append(pl.BlockSpec((None, slab, cols), lambda bi, i: (cast_layer, bi * nt + i, 0)))
        out_specs.append(pl.BlockSpec((slab, cols), lambda bi, i: (bi * nt + i, 0)))
        out_shape.append(jax.ShapeDtypeStruct((rows, cols), BF16))
    out, *casted = pl.pallas_call(
        functools.partial(_mix_out_kernel, n_cast=len(cast_f32)),
        grid=(b, nt),
        in_specs=in_specs,
        out_specs=out_specs,
        out_shape=out_shape,
        compiler_params=_cparams("parallel", "parallel"),
        name="mix_out",
    )(y, q_mem, mem_kv, w_out, x, *cast_f32)
    return out, casted


def _pack_attn_in(w_in):
    cuts = [Q_LORA, Q_LORA + KV_LORA, Q_LORA + KV_LORA + IDX_DIM, Q_LORA + KV_LORA + IDX_DIM + IDX_HEADS]
    c_q, c_kv, k_idx, w_idx, q_mem = jnp.split(w_in, cuts, axis=-1)
    pad = jnp.zeros(w_in.shape[:-1] + (ATTN_IN_PAD - w_in.shape[-1],), w_in.dtype)
    return jnp.concatenate([c_q, c_kv, q_mem, k_idx, w_idx, pad], axis=-1)


def _pad_lanes(v):
    return jnp.pad(v, (0, LANES - v.shape[0])).reshape(1, LANES)


def kernel(x, mem, mixer_norm, ffn_norm, final_norm, mem_norm, w_mem_kv, w_out, w_gate, w_up, w_down,
           w_in_attn, q_norm, kv_norm, w_uq, w_uk, w_uv, w_idx_uq, idx_k_norm, idx_k_bias,
           w_in_pool, w_pool_group, pool_scale):
    b, t, d = x.shape
    depth = w_out.shape[0]
    n_mem = mem.shape[1]
    row = lambda v: v.reshape(1, -1)
    assert t % DSA_TK == 0 and DSA_TK % DSA_TQ == 0 and t % MIX_TM == 0 and (b * t) % FFN_TM == 0
    assert w_gate.shape[2] % FFN_TF == 0 and (b * n_mem) % (FFN_TM // 2) == 0

    w_kv_b, w_out_b = w_mem_kv.astype(BF16), w_out.astype(BF16)
    ffn_f32 = (w_gate, w_up, w_down)
    w_in_attn_b = _pack_attn_in(w_in_attn).astype(BF16)
    w_uq_t = jnp.swapaxes(w_uq, 1, 2).astype(BF16)
    w_iq_t = jnp.swapaxes(w_idx_uq, 1, 2).astype(BF16)
    w_uk_b, w_uv_b = w_uk.astype(BF16), w_uv.astype(BF16)
    w_in_pool_b, w_group_b = w_in_pool.astype(BF16), w_pool_group.astype(BF16)

    mem_kv = rms_matmul_layers(mem.reshape(b * n_mem, d), row(mem_norm), w_kv_b, tm=FFN_TM // 2, out_dtype=BF16)
    mem_kv = mem_kv.reshape(depth, b, n_mem, 2 * MEM_WIDTH)

    for i in range(depth):
        j = i // 2
        if i % 2 == 0:
            ckv, ckv_t, kidx, q_mem, q_abs_t, q_idx_t, w_idx_t = attn_proj(
                x, row(mixer_norm[i]), w_in_attn_b, row(q_norm[j]), row(kv_norm[j]),
                _pad_lanes(idx_k_norm[j]), _pad_lanes(idx_k_bias[j]), w_uq_t, w_uk_b, w_iq_t, j,
                tm=max(MIX_TM, DSA_TQ), tq=DSA_TQ)
            y = dsa_attention(q_idx_t, w_idx_t, q_abs_t, kidx, ckv, ckv_t, w_uv_b, j, tq=DSA_TQ, tk=DSA_TK)
        else:
            y, q_mem = pool_layer(x, row(mixer_norm[i]), w_in_pool_b, w_group_b, row(pool_scale[j]), j, tm=MIX_TM)
        x, casted = mix_out(y, q_mem, mem_kv, w_out_b, i, x, ffn_f32 if i == 0 else (), tm=MIX_TM)
        if casted:
            ffn_w = casted
        last = i == depth - 1
        x, ffn_w = ffn(x.reshape(b * t, d), row(ffn_norm[i]), *ffn_w, row(final_norm),
                       None if last else (*ffn_f32, i + 1), tm=FFN_TM, tf=FFN_TF, final_norm=last)
        x = x.reshape(b, t, d)
    return x
```

```python
import functools

import jax
import jax.numpy as jnp
from jax import lax
from jax.experimental import pallas as pl
from jax.experimental.pallas import tpu as pltpu

BF16 = jnp.bfloat16
F32 = jnp.float32
I32 = jnp.int32
I16 = jnp.int16

EPS = 1e-6
HEAD_DIM = 128
N_ATTN_HEADS = 12
Q_LORA = 512
KV_LORA = 256
IDX_HEADS = 16
IDX_DIM = 64
TOPK_MAX = 256
POOL_WINDOWS = (2, 4, 8, 16)
POOL_GROUP = 384
POOL_WIDTH = 1536
MEM_WIDTH = 512
N_MEM_HEADS = 4
POOL_HALO = 32
SUBLANES = 8

LANES = 128
_CQ0, _CKV0, _QM0, _KI0 = 0, Q_LORA, Q_LORA + KV_LORA, Q_LORA + KV_LORA + MEM_WIDTH
ATTN_IN_PAD = _KI0 + LANES

LOG2E = 1.4426950408889634
INT_MIN = -(2 ** 31)
INT_MAX = 2 ** 31 - 1
I16_MIN = -(2 ** 15)
ONES_ROWS = 16
NEG = -1e30

VMEM_LIMIT = 60 * 1024 * 1024
MXU_COLS = 256
DSA_TQ = 256
DSA_TK = 512
FFN_TM, FFN_TF = 1024, 512
MIX_TM = 512


def _cparams(*sem):
    return pltpu.CompilerParams(dimension_semantics=sem, vmem_limit_bytes=VMEM_LIMIT)


def _multi_chain_sum(parts, chains=4):
    accs = list(parts[:chains])
    for n, part in enumerate(parts[chains:]):
        accs[n % chains] = accs[n % chains] + part
    while len(accs) > 1:
        accs = [a + b for a, b in zip(accs[0::2], accs[1::2])] + accs[len(accs) & ~1:]
    return accs[0]


def _loop_by_two(n, body, init):
    def two(p, carry):
        return body(2 * p + 1, body(2 * p, carry))
    carry = lax.fori_loop(0, n // 2, two, init)
    return lax.fori_loop((n // 2) * 2, n, body, carry)


def _rms(xf, g):
    ms = jnp.mean(xf * xf, axis=-1, keepdims=True)
    return xf * lax.rsqrt(ms + EPS) * g


def _rms_matmul_kernel(x_ref, g_ref, w_ref, o_ref):
    h = _rms(x_ref[...], g_ref[...]).astype(BF16)
    o_ref[...] = jnp.dot(h, w_ref[...], preferred_element_type=F32).astype(o_ref.dtype)


def rms_matmul_layers(x, g, w, *, tm, out_dtype):
    m, d = x.shape
    n_layers, _, n = w.shape
    return pl.pallas_call(
        _rms_matmul_kernel,
        grid=(m // tm, n_layers),
        in_specs=[pl.BlockSpec((tm, d), lambda i, l: (i, 0)),
                  pl.BlockSpec((1, d), lambda i, l: (0, 0)),
                  pl.BlockSpec((None, d, n), lambda i, l: (l, 0, 0))],
        out_specs=pl.BlockSpec((None, tm, n), lambda i, l: (l, i, 0)),
        out_shape=jax.ShapeDtypeStruct((n_layers, m, n), out_dtype),
        compiler_params=_cparams("parallel", "parallel"),
        name="rms_matmul_layers",
    )(x, g, w)


def _ffn_kernel(x_ref, g_ref, wg_ref, wu_ref, wd_ref, fg_ref, *rest, final_norm, cast_next):
    if cast_next:
        ng_ref, nu_ref, nd_ref, o_ref, cg_ref, cu_ref, cd_ref, h_scr = rest
        cg_ref[...] = ng_ref[...].astype(BF16)
        cu_ref[...] = nu_ref[...].astype(BF16)
        cd_ref[...] = nd_ref[...].astype(BF16)
    else:
        o_ref, h_scr = rest
    f = pl.program_id(1)

    @pl.when(f == 0)
    def _():
        x = x_ref[...]
        h_scr[...] = _rms(x, g_ref[...]).astype(BF16)
        o_ref[...] = x

    h = h_scr[...]
    gate = jnp.dot(h, wg_ref[...], preferred_element_type=F32)
    up = jnp.dot(h, wu_ref[...], preferred_element_type=F32)
    act = (gate * jax.nn.sigmoid(gate) * up).astype(BF16)
    o_ref[...] += jnp.dot(act, wd_ref[...], preferred_element_type=F32)

    if final_norm:
        @pl.when(f == pl.num_programs(1) - 1)
        def _():
            o_ref[...] = _rms(o_ref[...], fg_ref[...])


def ffn(x, g, w_gate, w_up, w_down, final_g, next_f32, *, tm, tf, final_norm):
    m, d = x.shape
    dff = w_gate.shape[1]
    nm, nf = m // tm, dff // tf
    in_specs = [pl.BlockSpec((tm, d), lambda i, f: (i, 0)),
                pl.BlockSpec((1, d), lambda i, f: (0, 0)),
                pl.BlockSpec((d, tf), lambda i, f: (0, f)),
                pl.BlockSpec((d, tf), lambda i, f: (0, f)),
                pl.BlockSpec((tf, d), lambda i, f: (f, 0)),
                pl.BlockSpec((1, d), lambda i, f: (0, 0))]
    out_specs = [pl.BlockSpec((tm, d), lambda i, f: (i, 0))]
    out_shape = [jax.ShapeDtypeStruct((m, d), F32)]
    args = [x, g, w_gate, w_up, w_down, final_g]
    if next_f32 is not None:
        *stacked, layer = next_f32
        assert d % nm == 0 and (d // nm) % (2 * SUBLANES) == 0
        ds = d // nm
        in_specs += [pl.BlockSpec((None, ds, tf), lambda i, f: (layer, i, f)),
                     pl.BlockSpec((None, ds, tf), lambda i, f: (layer, i, f)),
                     pl.BlockSpec((None, tf, ds), lambda i, f: (layer, f, i))]
        out_specs += [pl.BlockSpec((ds, tf), lambda i, f: (i, f)),
                      pl.BlockSpec((ds, tf), lambda i, f: (i, f)),
                      pl.BlockSpec((tf, ds), lambda i, f: (f, i))]
        out_shape += [jax.ShapeDtypeStruct(w.shape[1:], BF16) for w in stacked]
        args += stacked
    out, *casted = pl.pallas_call(
        functools.partial(_ffn_kernel, final_norm=final_norm, cast_next=next_f32 is not None),
        grid=(nm, nf),
        in_specs=in_specs,
        out_specs=out_specs,
        out_shape=out_shape,
        scratch_shapes=[pltpu.VMEM((tm, d), BF16)],
        compiler_params=_cparams("parallel", "arbitrary"),
        name="ffn",
    )(*args)
    return out, casted


def _attn_proj_kernel(x_ref, g_ref, win_ref, qn_ref, kvn_ref, ig_ref, ib_ref, wuqt_ref, wuk_ref, wiqt_ref,
                      ckv_ref, ckvt_ref, kidx_ref, qmem_ref, qabst_ref, qidxt_ref, widxt_ref, *, tq):
    tm = x_ref.shape[1]
    h = _rms(x_ref[0], g_ref[...]).astype(BF16)
    proj = jnp.dot(h, win_ref[...], preferred_element_type=F32)
    c_q_t = _rms(proj[:, _CQ0:_CQ0 + Q_LORA], qn_ref[...]).T.astype(BF16)
    c_kv = _rms(proj[:, _CKV0:_CKV0 + KV_LORA], kvn_ref[...])
    ckv_ref[0] = c_kv.astype(BF16)
    ckvt_ref[0, 0:KV_LORA, :] = c_kv.T.astype(BF16)
    ckvt_ref[0, KV_LORA:KV_LORA + ONES_ROWS, :] = jnp.ones((ONES_ROWS, tm), BF16)
    qmem_ref[0] = proj[:, _QM0:_QM0 + MEM_WIDTH].astype(BF16)
    slab = proj[:, _KI0:_KI0 + LANES]
    live = lax.broadcasted_iota(I32, slab.shape, 1) < IDX_DIM
    mu = jnp.sum(jnp.where(live, slab, 0.0), axis=-1, keepdims=True) * (1.0 / IDX_DIM)
    cen = jnp.where(live, slab - mu, 0.0)
    var = jnp.sum(cen * cen, axis=-1, keepdims=True) * (1.0 / IDX_DIM)
    kin = cen * lax.rsqrt(var + EPS) * ig_ref[...] + ib_ref[...]
    kidx_ref[0] = kin[:, :IDX_DIM].astype(BF16)
    widxt_ref[0] = slab.T[IDX_DIM:IDX_DIM + IDX_HEADS, :] * (IDX_HEADS ** -0.5 * IDX_DIM ** -0.5)
    q_t = jnp.dot(wuqt_ref[...], c_q_t, preferred_element_type=F32).astype(BF16)
    for hd in range(N_ATTN_HEADS):
        qa_t = jnp.dot(wuk_ref[hd], q_t[hd * HEAD_DIM:(hd + 1) * HEAD_DIM, :], preferred_element_type=F32)
        qa_t = (qa_t * (HEAD_DIM ** -0.5 * LOG2E)).astype(BF16)
        for j in range(tm // tq):
            qabst_ref[0, j, :, hd * tq:(hd + 1) * tq] = qa_t[:, j * tq:(j + 1) * tq]
    qi_t = jnp.dot(wiqt_ref[...], c_q_t, preferred_element_type=F32).astype(BF16)
    for hd in range(IDX_HEADS):
        for j in range(tm // tq):
            qidxt_ref[0, j, :, hd * tq:(hd + 1) * tq] = qi_t[hd * IDX_DIM:(hd + 1) * IDX_DIM, j * tq:(j + 1) * tq]


def attn_proj(x, g, w_in, q_norm, kv_norm, idx_g, idx_b, w_uq_t, w_uk, w_iq_t, layer, *, tm, tq):
    b, t, d = x.shape
    full = lambda shape: pl.BlockSpec(shape, lambda bi, i: (0,) * len(shape))
    stacked = lambda w: pl.BlockSpec((None,) + w.shape[1:], lambda bi, i: (layer,) + (0,) * (w.ndim - 1))
    per_tile = tm // tq
    return pl.pallas_call(
        functools.partial(_attn_proj_kernel, tq=tq),
        grid=(b, t // tm),
        in_specs=[pl.BlockSpec((1, tm, d), lambda bi, i: (bi, i, 0)),
                  full((1, d)), stacked(w_in), full((1, Q_LORA)), full((1, KV_LORA)),
                  full((1, LANES)), full((1, LANES)), stacked(w_uq_t), stacked(w_uk), stacked(w_iq_t)],
        out_specs=[pl.BlockSpec((1, tm, KV_LORA), lambda bi, i: (bi, i, 0)),
                   pl.BlockSpec((1, KV_LORA + ONES_ROWS, tm), lambda bi, i: (bi, 0, i)),
                   pl.BlockSpec((1, tm, IDX_DIM), lambda bi, i: (bi, i, 0)),
                   pl.BlockSpec((1, tm, MEM_WIDTH), lambda bi, i: (bi, i, 0)),
                   pl.BlockSpec((1, per_tile, KV_LORA, N_ATTN_HEADS * tq), lambda bi, i: (bi, i, 0, 0)),
                   pl.BlockSpec((1, per_tile, IDX_DIM, IDX_HEADS * tq), lambda bi, i: (bi, i, 0, 0)),
                   pl.BlockSpec((1, IDX_HEADS, tm), lambda bi, i: (bi, 0, i))],
        out_shape=[jax.ShapeDtypeStruct((b, t, KV_LORA), BF16),
                   jax.ShapeDtypeStruct((b, KV_LORA + ONES_ROWS, t), BF16),
                   jax.ShapeDtypeStruct((b, t, IDX_DIM), BF16),
                   jax.ShapeDtypeStruct((b, t, MEM_WIDTH), BF16),
                   jax.ShapeDtypeStruct((b, t // tq, KV_LORA, N_ATTN_HEADS * tq), BF16),
                   jax.ShapeDtypeStruct((b, t // tq, IDX_DIM, IDX_HEADS * tq), BF16),
                   jax.ShapeDtypeStruct((b, IDX_HEADS, t), F32)],
        compiler_params=_cparams("parallel", "parallel"),
        name="attn_proj",
    )(x, g, w_in, q_norm, kv_norm, idx_g, idx_b, w_uq_t, w_uk, w_iq_t)


def _dsa_kernel(qidxt_ref, widxt_ref, qabst_ref, kidx_ref, ckv_ref, ckvt_ref, wuv_ref, y_ref,
                key_scr, half_scr, m_scr, acc_scr, *, tq, tk, k_top):
    assert tk in (tq, 2 * tq)
    i = pl.program_id(1)
    q0 = i * tq
    n_full = (q0 + tq) // tk
    has_tail = jnp.where((q0 + tq) % tk > 0, 1, 0)
    qpos = q0 + lax.broadcasted_iota(I32, (1, tq), 1)
    hpt = max(1, MXU_COLS // tq)

    def over_chunks(body, init, by_two=False):
        def full_body(c, carry):
            return body(pl.multiple_of(c * tk, tk), tk, carry)
        if by_two:
            carry = _loop_by_two(n_full, full_body, init)
        else:
            carry = lax.fori_loop(0, n_full, full_body, init)
        if tk == tq:
            return carry
        return lax.fori_loop(0, has_tail, lambda _, cy: body(pl.multiple_of(n_full * tk, tk), tq, cy), carry)

    w = widxt_ref[0]

    def score_body(k0, rows, carry):
        kc = kidx_ref[0, pl.ds(k0, rows), :]
        score = None
        for j in range(IDX_HEADS // hpt):
            s = jnp.dot(kc, qidxt_ref[0, 0, :, j * hpt * tq:(j + 1) * hpt * tq], preferred_element_type=F32)
            for u in range(hpt):
                hd = hpt * j + u
                term = jnp.maximum(s[:, u * tq:(u + 1) * tq], 0.0) * w[hd:hd + 1, :]
                score = term if score is None else score + term
        score = jnp.where(score == 0.0, 0.0, score)
        bits = pltpu.bitcast(score, I32)
        key = bits ^ ((bits >> 31) & 0x7FFFFFFF)
        causal = k0 + lax.broadcasted_iota(I32, (rows, 1), 0) <= qpos
        key_scr[pl.ds(k0, rows), :] = jnp.where(causal, key, INT_MIN)
        half_scr[pl.ds(k0, rows), :] = (jnp.where(causal, key, INT_MIN) >> 16).astype(I16)
        return carry

    over_chunks(score_body, 0, by_two=True)

    def count_rows(hit_fn):
        def body(k0, rows, acc):
            kpos = k0 + lax.broadcasted_iota(I32, (rows, 1), 0)
            hit = jnp.where(hit_fn(key_scr[pl.ds(k0, rows), :], kpos), 1, 0).reshape(rows // 8, 8, tq)
            return acc + _multi_chain_sum([hit[g] for g in range(rows // 8)])
        return jnp.sum(over_chunks(body, jnp.zeros((8, tq), I32)), axis=0, keepdims=True)

    def count_ge(cand):
        return count_rows(lambda kk, kpos: kk >= cand)

    def count_half(hit_fn):
        one, zero = jnp.ones((), I16), jnp.zeros((), I16)

        def body(k0, rows, acc):
            hit = jnp.where(hit_fn(half_scr[pl.ds(k0, rows), :]), one, zero).reshape(rows // 16, 16, tq)
            part = _multi_chain_sum([hit[g] for g in range(rows // 16)])
            return acc + part.astype(I32)
        return jnp.sum(over_chunks(body, jnp.zeros((16, tq), I32), by_two=True), axis=0, keepdims=True)

    def half_search(base):
        def body(b, state):
            v, cnt = state
            cand = v + lax.shift_left(jnp.int32(1), 15 - b)
            c = base + count_half(lambda hh: hh >= cand.astype(I16))
            take = c >= k_top
            return jnp.where(take, cand, v), jnp.where(take, c, cnt)
        return lax.fori_loop(0, 16, body, (jnp.full((1, tq), I16_MIN, I32), jnp.full((1, tq), -1, I32)))

    top, cnt_top = half_search(jnp.zeros((1, tq), I32))
    above = count_half(lambda hh: hh > top.astype(I16))

    def low_body(k0, rows, carry):
        kk = key_scr[pl.ds(k0, rows), :]
        low = jnp.where((kk >> 16) == top, (kk & 0xFFFF) + I16_MIN, I16_MIN)
        half_scr[pl.ds(k0, rows), :] = low.astype(I16)
        return carry

    over_chunks(low_body, 0)
    low, cnt_low = half_search(above)
    thr = top * 65536 + (low - I16_MIN)
    cnt = jnp.where(cnt_low >= 0, cnt_low, cnt_top)

    @pl.when(jnp.sum(jnp.where(cnt > k_top, 1, 0)) > 0)
    def _():
        at_max = thr == INT_MAX
        above = jnp.where(at_max, 0, count_ge(jnp.where(at_max, thr, thr + 1)))
        need = k_top - above
        pos_bits = max(1, (key_scr.shape[0] - 1).bit_length())

        def pos_body(b, lo):
            cand = lo + lax.shift_left(jnp.int32(1), pos_bits - 1 - b)
            before = count_rows(lambda kk, kpos: (kk == thr) & (kpos < cand))
            return jnp.where(before < need, cand, lo)

        last = lax.fori_loop(0, pos_bits, pos_body, jnp.zeros((1, tq), I32))

        def drop_body(k0, rows, carry):
            kpos = k0 + lax.broadcasted_iota(I32, (rows, 1), 0)
            kk = key_scr[pl.ds(k0, rows), :]
            key_scr[pl.ds(k0, rows), :] = jnp.where((kk == thr) & (kpos > last), INT_MIN, kk)
            return carry

        over_chunks(drop_body, 0)

    thr = jnp.maximum(thr, INT_MIN + 1)

    m_scr[...] = jnp.full(m_scr.shape, NEG, F32)
    acc_scr[...] = jnp.zeros(acc_scr.shape, F32)

    n_tiles = N_ATTN_HEADS // hpt
    tile_w = hpt * tq

    def softmax_tile(j, kv, bias):
        lt = jnp.dot(kv, qabst_ref[0, 0, :, j * tile_w:(j + 1) * tile_w], preferred_element_type=F32)
        ps, alphas = [], []
        for u in range(hpt):
            hc = slice((hpt * j + u) * tq, (hpt * j + u + 1) * tq)
            lg = lt[:, u * tq:(u + 1) * tq] + bias
            m_prev = m_scr[:, hc]
            m_new = jnp.maximum(m_prev, jnp.max(lg, axis=0, keepdims=True))
            ps.append(jnp.exp2(lg - m_new).astype(BF16))
            alphas.append(jnp.exp2(m_prev - m_new))
            m_scr[:, hc] = m_new
        return jnp.concatenate(ps, axis=1), jnp.concatenate(alphas, axis=1)

    def accumulate(j, kv_t, p, alpha):
        cols = slice(j * tile_w, (j + 1) * tile_w)
        acc_scr[:, cols] = alpha * acc_scr[:, cols] + jnp.dot(kv_t, p, preferred_element_type=F32)

    def att_body(k0, rows, carry):
        kv = ckv_ref[0, pl.ds(k0, rows), :]
        kv_t = ckvt_ref[0, :, pl.ds(k0, rows)]
        bias = jnp.where(key_scr[pl.ds(k0, rows), :] >= thr, 0.0, NEG)
        for j in range(n_tiles):
            accumulate(j, kv_t, *softmax_tile(j, kv, bias))
        return carry

    over_chunks(att_body, 0)

    for hd in range(N_ATTN_HEADS):
        hc = slice(hd * tq, (hd + 1) * tq)
        den = acc_scr[KV_LORA:KV_LORA + 1, hc]
        o_lat = (acc_scr[0:KV_LORA, hc] * (1.0 / den)).T.astype(BF16)
        y_ref[0, :, hd * HEAD_DIM:(hd + 1) * HEAD_DIM] = jnp.dot(
            o_lat, wuv_ref[hd], preferred_element_type=F32).astype(y_ref.dtype)


def dsa_attention(q_idx_t, w_idx_t, q_abs_t, k_idx, c_kv, c_kv_t, w_uv, layer, *, tq, tk):
    b, t, _ = c_kv.shape
    k_top = min(TOPK_MAX, t // 4)
    return pl.pallas_call(
        functools.partial(_dsa_kernel, tq=tq, tk=tk, k_top=k_top),
        grid=(b, t // tq),
        in_specs=[pl.BlockSpec((1, 1, IDX_DIM, IDX_HEADS * tq), lambda bi, i: (bi, i, 0, 0)),
                  pl.BlockSpec((1, IDX_HEADS, tq), lambda bi, i: (bi, 0, i)),
                  pl.BlockSpec((1, 1, KV_LORA, N_ATTN_HEADS * tq), lambda bi, i: (bi, i, 0, 0)),
                  pl.BlockSpec((1, t, IDX_DIM), lambda bi, i: (bi, 0, 0)),
                  pl.BlockSpec((1, t, KV_LORA), lambda bi, i: (bi, 0, 0)),
                  pl.BlockSpec((1, KV_LORA + ONES_ROWS, t), lambda bi, i: (bi, 0, 0)),
                  pl.BlockSpec((None,) + w_uv.shape[1:], lambda bi, i: (layer, 0, 0, 0))],
        out_specs=pl.BlockSpec((1, tq, N_ATTN_HEADS * HEAD_DIM), lambda bi, i: (bi, i, 0)),
        out_shape=jax.ShapeDtypeStruct((b, t, N_ATTN_HEADS * HEAD_DIM), BF16),
        scratch_shapes=[pltpu.VMEM((t, tq), I32),
                        pltpu.VMEM((t, tq), I16),
                        pltpu.VMEM((1, N_ATTN_HEADS * tq), F32),
                        pltpu.VMEM((KV_LORA + ONES_ROWS, N_ATTN_HEADS * tq), F32)],
        compiler_params=_cparams("parallel", "parallel"),
        name="dsa_attention",
    )(q_idx_t, w_idx_t, q_abs_t, k_idx, c_kv, c_kv_t, w_uv)


def _pool_kernel(x_ref, xh_ref, g_ref, win_ref, wg_ref, sc_ref, y_ref, qm_ref, ext_scr, a_scr, b_scr, *, tm):
    i = pl.program_id(1)
    rows = jnp.concatenate([xh_ref[0], x_ref[0]], axis=0)
    proj = jnp.dot(_rms(rows, g_ref[...]).astype(BF16), win_ref[...], preferred_element_type=F32)
    qm_ref[0] = proj[POOL_HALO:, POOL_WIDTH:].astype(qm_ref.dtype)
    ext_scr[...] = proj[:, :POOL_WIDTH]

    @pl.when(i == 0)
    def _():
        ext_scr[0:POOL_HALO, :] = jnp.zeros((POOL_HALO, POOL_WIDTH), F32)

    n_rows = POOL_HALO + tm
    pos = i * tm + lax.broadcasted_iota(I32, (tm, 1), 0)
    for g, win in enumerate(POOL_WINDOWS):
        lo = g * POOL_GROUP
        tok = ext_scr[POOL_HALO:POOL_HALO + tm, lo:lo + POOL_GROUP]
        src, width, start = ext_scr.at[:, lo:lo + POOL_GROUP], 1, 0
        for dst in (a_scr, b_scr, a_scr)[:max(0, win.bit_length() - 2)]:
            start += SUBLANES
            dst[start:, :] = src[start:, :] + src[start - width:n_rows - width, :]
            src, width = dst, 2 * width
        total = src[POOL_HALO:, :] + src[POOL_HALO - width:n_rows - width, :]
        cnt = jnp.minimum(pos + 1, win).astype(F32)
        mixed = (total / cnt - tok).astype(BF16)
        out = jnp.dot(mixed, wg_ref[g], preferred_element_type=F32)
        y_ref[0, :, lo:lo + POOL_GROUP] = (out * sc_ref[:, lo:lo + POOL_GROUP]).astype(y_ref.dtype)


def pool_layer(x, g, w_in, w_group, scale, layer, *, tm):
    b, t, d = x.shape
    halo_blocks = tm // POOL_HALO
    return pl.pallas_call(
        functools.partial(_pool_kernel, tm=tm),
        grid=(b, t // tm),
        in_specs=[pl.BlockSpec((1, tm, d), lambda bi, i: (bi, i, 0)),
                  pl.BlockSpec((1, POOL_HALO, d), lambda bi, i: (bi, jnp.maximum(i * halo_blocks - 1, 0), 0)),
                  pl.BlockSpec((1, d), lambda bi, i: (0, 0)),
                  pl.BlockSpec(w_in.shape, lambda bi, i: (0, 0)),
                  pl.BlockSpec((None,) + w_group.shape[1:], lambda bi, i: (layer, 0, 0, 0)),
                  pl.BlockSpec((1, POOL_WIDTH), lambda bi, i: (0, 0))],
        out_specs=[pl.BlockSpec((1, tm, POOL_WIDTH), lambda bi, i: (bi, i, 0)),
                   pl.BlockSpec((1, tm, MEM_WIDTH), lambda bi, i: (bi, i, 0))],
        out_shape=[jax.ShapeDtypeStruct((b, t, POOL_WIDTH), BF16),
                   jax.ShapeDtypeStruct((b, t, MEM_WIDTH), BF16)],
        scratch_shapes=[pltpu.VMEM((POOL_HALO + tm, POOL_WIDTH), F32),
                        pltpu.VMEM((POOL_HALO + tm, POOL_GROUP), F32),
                        pltpu.VMEM((POOL_HALO + tm, POOL_GROUP), F32)],
        compiler_params=_cparams("parallel", "parallel"),
        name="pool_layer",
    )(x, x, g, w_in, w_group, scale)


def _mix_out_kernel(y_ref, qm_ref, mkv_ref, wo_ref, x_ref, *rest, n_cast):
    o_ref = rest[n_cast]
    for src_ref, dst_ref in zip(rest[:n_cast], rest[n_cast + 1:]):
        dst_ref[...] = src_ref[...].astype(BF16)
    qm = qm_ref[0]
    mkv = mkv_ref[0]
    parts = [y_ref[0]]
    for hd in range(N_MEM_HEADS):
        lo = hd * HEAD_DIM
        logits = lax.dot_general(qm[:, lo:lo + HEAD_DIM], mkv[:, lo:lo + HEAD_DIM],
                                 (((1,), (1,)), ((), ())), preferred_element_type=F32) * HEAD_DIM ** -0.5
        p = jnp.exp(logits - jnp.max(logits, axis=-1, keepdims=True))
        den = jnp.sum(p, axis=-1, keepdims=True)
        att = jnp.dot(p.astype(BF16), mkv[:, MEM_WIDTH + lo:MEM_WIDTH + lo + HEAD_DIM],
                      preferred_element_type=F32)
        parts.append((att / den).astype(BF16))
    mix = jnp.concatenate(parts, axis=-1)
    o_ref[0] = x_ref[0] + jnp.dot(mix, wo_ref[...], preferred_element_type=F32)


def mix_out(y, q_mem, mem_kv, layer, w_out, x, cast_f32=(), *, tm):
    b, t, d = x.shape
    nt = t // tm
    in_specs = [pl.BlockSpec((1, tm, y.shape[-1]), lambda bi, i: (bi, i, 0)),
                pl.BlockSpec((1, tm, MEM_WIDTH), lambda bi, i: (bi, i, 0)),
                pl.BlockSpec((None, 1) + mem_kv.shape[2:], lambda bi, i: (layer, bi, 0, 0)),
                pl.BlockSpec(w_out.shape, lambda bi, i: (0, 0)),
                pl.BlockSpec((1, tm, d), lambda bi, i: (bi, i, 0))]
    out_specs = [pl.BlockSpec((1, tm, d), lambda bi, i: (bi, i, 0))]
    out_shape = [jax.ShapeDtypeStruct((b, t, d), F32)]
    for w, cast_layer in cast_f32:
        _, rows, cols = w.shape
        assert rows % (b * nt) == 0 and (rows // (b * nt)) % (2 * SUBLANES) == 0
        slab = rows // (b * nt)
        in_specs.append(pl.BlockSpec((None, slab, cols), lambda bi, i, cl=cast_layer: (cl, bi * nt + i, 0)))
        out_specs.append(pl.BlockSpec((slab, cols), lambda bi, i: (bi * nt + i, 0)))
        out_shape.append(jax.ShapeDtypeStruct((rows, cols), BF16))
    out, *casted = pl.pallas_call(
        functools.partial(_mix_out_kernel, n_cast=len(cast_f32)),
        grid=(b, nt),
        in_specs=in_specs,
        out_specs=out_specs,
        out_shape=out_shape,
        compiler_params=_cparams("parallel", "parallel"),
        name="mix_out",
    )(y, q_mem, mem_kv, w_out, x, *[w for w, _ in cast_f32])
    return out, casted


def _pack_attn_in(w_in):
    cuts = [Q_LORA, Q_LORA + KV_LORA, Q_LORA + KV_LORA + IDX_DIM, Q_LORA + KV_LORA + IDX_DIM + IDX_HEADS]
    c_q, c_kv, k_idx, w_idx, q_mem = jnp.split(w_in, cuts, axis=-1)
    pad = jnp.zeros(w_in.shape[:-1] + (ATTN_IN_PAD - w_in.shape[-1],), w_in.dtype)
    return jnp.concatenate([c_q, c_kv, q_mem, k_idx, w_idx, pad], axis=-1)


def _pad_lanes(v):
    return jnp.pad(v, (0, LANES - v.shape[0])).reshape(1, LANES)


def kernel(x, mem, mixer_norm, ffn_norm, final_norm, mem_norm, w_mem_kv, w_out, w_gate, w_up, w_down,
           w_in_attn, q_norm, kv_norm, w_uq, w_uk, w_uv, w_idx_uq, idx_k_norm, idx_k_bias,
           w_in_pool, w_pool_group, pool_scale):
    b, t, d = x.shape
    depth = w_out.shape[0]
    n_mem = mem.shape[1]
    row = lambda v: v.reshape(1, -1)
    assert t % DSA_TK == 0 and DSA_TK % DSA_TQ == 0 and t % MIX_TM == 0 and (b * t) % FFN_TM == 0
    assert w_gate.shape[2] % FFN_TF == 0 and (b * n_mem) % (FFN_TM // 2) == 0

    w_kv_b = w_mem_kv.astype(BF16)
    ffn_f32 = (w_gate, w_up, w_down)
    w_out_l, w_in_pool_l = w_out[0].astype(BF16), None
    w_in_attn_b = _pack_attn_in(w_in_attn).astype(BF16)
    w_uq_t = jnp.swapaxes(w_uq, 1, 2).astype(BF16)
    w_iq_t = jnp.swapaxes(w_idx_uq, 1, 2).astype(BF16)
    w_uk_b, w_uv_b = w_uk.astype(BF16), w_uv.astype(BF16)
    w_group_b = w_pool_group.astype(BF16)

    mem_kv = rms_matmul_layers(mem.reshape(b * n_mem, d), row(mem_norm), w_kv_b, tm=FFN_TM // 2, out_dtype=BF16)
    mem_kv = mem_kv.reshape(depth, b, n_mem, 2 * MEM_WIDTH)

    for i in range(depth):
        j = i // 2
        if i % 2 == 0:
            ckv, ckv_t, kidx, q_mem, q_abs_t, q_idx_t, w_idx_t = attn_proj(
                x, row(mixer_norm[i]), w_in_attn_b, row(q_norm[j]), row(kv_norm[j]),
                _pad_lanes(idx_k_norm[j]), _pad_lanes(idx_k_bias[j]), w_uq_t, w_uk_b, w_iq_t, j,
                tm=max(MIX_TM, DSA_TQ), tq=DSA_TQ)
            y = dsa_attention(q_idx_t, w_idx_t, q_abs_t, kidx, ckv, ckv_t, w_uv_b, j, tq=DSA_TQ, tk=DSA_TK)
        else:
            y, q_mem = pool_layer(x, row(mixer_norm[i]), w_in_pool_l, w_group_b, row(pool_scale[j]), j, tm=MIX_TM)
        last = i == depth - 1
        casts = [(w, 0) for w in ffn_f32] if i == 0 else []
        if not last:
            casts.append((w_out, i + 1))
        if i % 2 == 0:
            casts.append((w_in_pool, j))
        x, casted = mix_out(y, q_mem, mem_kv, i, w_out_l, x, casts, tm=MIX_TM)
        if i == 0:
            ffn_w, casted = casted[:3], casted[3:]
        if not last:
            w_out_l = casted.pop(0)
        if i % 2 == 0:
            w_in_pool_l = casted.pop(0)
        x, ffn_w = ffn(x.reshape(b * t, d), row(ffn_norm[i]), *ffn_w, row(final_norm),
                       None if last else (*ffn_f32, i + 1), tm=FFN_TM, tf=FFN_TF, final_norm=last)
        x = x.reshape(b, t, d)
    return x
```
